```python
import jax, jax.numpy as jnp
from jax import lax
import numpy as np

D_MODEL = 1024
BATCH = 4
SEQ = 8192
DEPTH = 2

GRID_W = 64
WIN_ROWS = 8
WIN_COLS = 16
N_ATTN_HEADS = 8
HEAD_DIM = 64
D_ATTN = N_ATTN_HEADS * HEAD_DIM
N_FOURIER_GROUPS = 4
FOURIER_GROUP = 128
D_FOURIER = N_FOURIER_GROUPS * FOURIER_GROUP
D_MIX = D_ATTN + D_FOURIER
D_IN = 3 * D_ATTN + D_FOURIER
D_FF = 2816
ALPHA = (2.0 * DEPTH) ** 0.25
BETA = (8.0 * DEPTH) ** -0.25
LN_EPS = 1e-5
RMS_EPS = 1e-6
NEG_INF = -1e30

kernel_name = "hybrid_natten_fnet_macaron_deepnorm"


def layer_norm(x, g, b):
    xf = x.astype(jnp.float32)
    mu = jnp.mean(xf, axis=-1, keepdims=True)
    xc = xf - mu
    var = jnp.mean(xc * xc, axis=-1, keepdims=True)
    y = xc * lax.rsqrt(var + LN_EPS) * g.astype(jnp.float32) + b.astype(jnp.float32)
    return y.astype(x.dtype)


def rms_norm(x, g):
    xf = x.astype(jnp.float32)
    y = xf * lax.rsqrt(jnp.mean(xf * xf, axis=-1, keepdims=True) + RMS_EPS)
    return y * g.astype(jnp.float32)


def swiglu(x, w_gate, w_up, w_down):
    return (jax.nn.silu(x @ w_gate) * (x @ w_up)) @ w_down


def neighbourhood_attention(q, k, v, rpb):
    B, S, H, Dh = q.shape
    R = S // GRID_W
    KH = min(WIN_ROWS, R)

    def to_grid(t):
        return t.reshape(B, R, GRID_W, H, Dh).transpose(0, 3, 1, 2, 4).astype(jnp.float32)

    rows = jnp.arange(R)
    r0 = jnp.clip(rows - KH // 2, 0, R - KH)
    row_idx = r0[:, None] + jnp.arange(KH)[None, :]
    dr = row_idx - rows[:, None]

    cols = jnp.arange(GRID_W)
    c0 = jnp.clip(cols - WIN_COLS // 2, 0, GRID_W - WIN_COLS)
    dc = cols[None, :] - cols[:, None]
    in_win = (cols[None, :] >= c0[:, None]) & (cols[None, :] < c0[:, None] + WIN_COLS)
    dc_idx = jnp.clip(dc, -(WIN_COLS - 1), WIN_COLS - 1) + (WIN_COLS - 1)
    dr_idx = dr + (WIN_ROWS - 1)

    bias = rpb.astype(jnp.float32)[:, dr_idx[:, None, :, None], dc_idx[None, :, None, :]]
    mask = jnp.where(in_win, 0.0, NEG_INF).astype(jnp.float32)[None, None, :, None, :]
    bias = bias + mask

    qg = to_grid(q) * (Dh ** -0.5)
    kg = to_grid(k)[:, :, row_idx]
    vg = to_grid(v)[:, :, row_idx]

    scores = jnp.einsum('bhrqd,bhrnkd->bhrqnk', qg, kg) + bias[None]
    p = jax.nn.softmax(scores.reshape(B, H, R, GRID_W, KH * GRID_W), axis=-1)
    p = p.reshape(scores.shape)
    o = jnp.einsum('bhrqnk,bhrnkd->bhrqd', p, vg)
    return o.transpose(0, 2, 3, 1, 4).reshape(B, S, H * Dh)


def fourier_mix(u):
    B, S, _ = u.shape
    ug = u.reshape(B, S, N_FOURIER_GROUPS, FOURIER_GROUP).astype(jnp.float32)
    y = jnp.fft.fft2(ug, axes=(1, 3), norm="ortho").real
    return y.reshape(B, S, D_FOURIER)


def setup_inputs(seed: int = 0) -> dict:
    key = jax.random.key(seed)
    ks = jax.random.split(key, 24)
    f32 = jnp.float32

    def nrm(k, shape, scale):
        return jax.random.normal(k, shape, f32) * scale

    def gain(k, shape):
        return 1.0 + 0.05 * jax.random.normal(k, shape, f32)

    def bias(k, shape):
        return 0.02 * jax.random.normal(k, shape, f32)

    L, D, F = DEPTH, D_MODEL, D_FF
    x = jax.random.normal(ks[0], (BATCH, SEQ, D), f32)

    w_in = nrm(ks[7], (L, D, D_IN), D ** -0.5)
    col_scale = jnp.concatenate([jnp.ones((2 * D_ATTN,), f32), jnp.full((D_ATTN,), BETA, f32),
                                 jnp.ones((D_FOURIER,), f32)])
    w_in = w_in * col_scale

    return {
        "x": x,
        "ffn1_w_gate": nrm(ks[1], (L, D, F), D ** -0.5),
        "ffn1_w_up": nrm(ks[2], (L, D, F), D ** -0.5),
        "ffn1_w_down": nrm(ks[3], (L, F, D), BETA * F ** -0.5),
        "ln1_g": gain(ks[4], (L, D)),
        "ln1_b": bias(ks[5], (L, D)),
        "w_in": w_in,
        "rpb": nrm(ks[8], (L, N_ATTN_HEADS, 2 * WIN_ROWS - 1, 2 * WIN_COLS - 1), 0.1),
        "g_attn": gain(ks[9], (L, D_ATTN)),
        "g_fourier": gain(ks[10], (L, D_FOURIER)),
        "w_out": nrm(ks[11], (L, D_MIX, D), BETA * D_MIX ** -0.5),
        "ln2_g": gain(ks[12], (L, D)),
        "ln2_b": bias(ks[13], (L, D)),
        "ffn2_w_gate": nrm(ks[14], (L, D, F), D ** -0.5),
        "ffn2_w_up": nrm(ks[15], (L, D, F), D ** -0.5),
        "ffn2_w_down": nrm(ks[16], (L, F, D), BETA * F ** -0.5),
        "ln3_g": gain(ks[17], (L, D)),
        "ln3_b": bias(ks[18], (L, D)),
    }


def reference(x, ffn1_w_gate, ffn1_w_up, ffn1_w_down, ln1_g, ln1_b, w_in, rpb,
              g_attn, g_fourier, w_out, ln2_g, ln2_b, ffn2_w_gate, ffn2_w_up,
              ffn2_w_down, ln3_g, ln3_b):
    B, S, D = x.shape
    for l in range(DEPTH):
        x = layer_norm(ALPHA * x + 0.5 * swiglu(x, ffn1_w_gate[l], ffn1_w_up[l], ffn1_w_down[l]),
                       ln1_g[l], ln1_b[l])

        proj = x @ w_in[l]
        q = proj[..., :D_ATTN].reshape(B, S, N_ATTN_HEADS, HEAD_DIM)
        k = proj[..., D_ATTN:2 * D_ATTN].reshape(B, S, N_ATTN_HEADS, HEAD_DIM)
        v = proj[..., 2 * D_ATTN:3 * D_ATTN].reshape(B, S, N_ATTN_HEADS, HEAD_DIM)
        u = proj[..., 3 * D_ATTN:]

        attn = neighbourhood_attention(q, k, v, rpb[l])
        four = fourier_mix(u)
        merged = jnp.concatenate([rms_norm(attn, g_attn[l]), rms_norm(four, g_fourier[l])],
                                 axis=-1).astype(x.dtype)
        x = layer_norm(ALPHA * x + merged @ w_out[l], ln2_g[l], ln2_b[l])

        x = layer_norm(ALPHA * x + 0.5 * swiglu(x, ffn2_w_gate[l], ffn2_w_up[l], ffn2_w_down[l]),
                       ln3_g[l], ln3_b[l])
    return x
```

```python
import functools
import math

import jax
import jax.numpy as jnp
import numpy as np
from jax import lax
from jax.experimental import pallas as pl
from jax.experimental.pallas import tpu as pltpu

D_MODEL = 1024
DEPTH = 2
GRID_W = 64
WIN_ROWS = 8
WIN_COLS = 16
N_ATTN_HEADS = 8
HEAD_DIM = 64
D_ATTN = N_ATTN_HEADS * HEAD_DIM
N_FOURIER_GROUPS = 4
FOURIER_GROUP = 128
D_FOURIER = N_FOURIER_GROUPS * FOURIER_GROUP
D_IN = 3 * D_ATTN + D_FOURIER
D_FF = 2816
ALPHA = (2.0 * DEPTH) ** 0.25
LN_EPS = 1e-5
RMS_EPS = 1e-6
NEG_INF = -1e30

V7X_LANES = 128
V7X_MXU_DIM = 256
V7X_VMEM_BYTES = 64 * 1024 * 1024

F32 = jnp.float32
BF16 = jnp.bfloat16


def _cparams(vmem_bytes, n_axes):
    return pltpu.CompilerParams(
        dimension_semantics=("arbitrary",) * n_axes,
        vmem_limit_bytes=min(int(vmem_bytes), V7X_VMEM_BYTES - (2 << 20)),
    )


def _const_spec(shape):
    nd = len(shape)
    return pl.BlockSpec(shape, lambda *_: (0,) * nd, pipeline_mode=pl.Buffered(1))


def _layer_norm(z, g, b):
    mu = jnp.mean(z, axis=-1, keepdims=True)
    zc = z - mu
    var = jnp.mean(zc * zc, axis=-1, keepdims=True)
    return zc * lax.rsqrt(var + LN_EPS) * g + b


def _rms_norm(z, g):
    return z * lax.rsqrt(jnp.mean(z * z, axis=-1, keepdims=True) + RMS_EPS) * g


def _dot(a, b):
    return jnp.dot(a, b, preferred_element_type=F32)


FFN_TM = 512
FFN_FC = V7X_MXU_DIM


def _ffn_ln_kernel(x_ref, wg_ref, wu_ref, wd_ref, g_ref, b_ref, o_ref, h_ref):
    x = x_ref[...]
    xb = x.astype(BF16)
    for c in range(D_FF // FFN_FC):
        cols = slice(c * FFN_FC, (c + 1) * FFN_FC)
        gate = _dot(xb, wg_ref[:, cols])
        up = _dot(xb, wu_ref[:, cols])
        h_ref[:, cols] = (gate * jax.nn.sigmoid(gate) * up).astype(BF16)
    y = _dot(h_ref[...], wd_ref[...])
    o_ref[...] = _layer_norm(ALPHA * x + 0.5 * y, g_ref[...], b_ref[...])


def _ffn_ln(x, wg, wu, wd, g, b):
    n = x.shape[0]
    tm = FFN_TM
    vmem = (3 * D_MODEL * D_FF * 2
            + 4 * tm * D_MODEL * 4
            + tm * D_FF * 2
            + 6 * tm * D_MODEL * 4)
    return pl.pallas_call(
        _ffn_ln_kernel,
        grid=(n // tm,),
        in_specs=[
            pl.BlockSpec((tm, D_MODEL), lambda i: (i, 0)),
            _const_spec((D_MODEL, D_FF)),
            _const_spec((D_MODEL, D_FF)),
            _const_spec((D_FF, D_MODEL)),
            _const_spec((1, D_MODEL)),
            _const_spec((1, D_MODEL)),
        ],
        out_specs=pl.BlockSpec((tm, D_MODEL), lambda i: (i, 0)),
        out_shape=jax.ShapeDtypeStruct((n, D_MODEL), F32),
        scratch_shapes=[pltpu.VMEM((tm, D_FF), BF16)],
        compiler_params=_cparams(vmem, 1),
        name="ffn_ln",
    )(x, wg, wu, wd, g, b)


PROJ_TM = 512


def _proj_kernel(x_ref, w_ref, cs_ref, q_ref, k_ref, v_ref, a_ref, b_ref):
    xb = x_ref[...].astype(BF16)
    q_ref[...] = (_dot(xb, w_ref[:, 0:D_ATTN]) * (HEAD_DIM ** -0.5)).astype(BF16)
    k_ref[...] = _dot(xb, w_ref[:, D_ATTN:2 * D_ATTN]).astype(BF16)
    v_ref[...] = _dot(xb, w_ref[:, 2 * D_ATTN:3 * D_ATTN]).astype(BF16)
    u = _dot(xb, w_ref[:, 3 * D_ATTN:]).astype(BF16)
    for grp in range(N_FOURIER_GROUPS):
        cols = slice(grp * FOURIER_GROUP, (grp + 1) * FOURIER_GROUP)
        ab = _dot(u[:, cols], cs_ref[...])
        a_ref[:, cols] = ab[:, :FOURIER_GROUP].astype(BF16)
        b_ref[:, cols] = ab[:, FOURIER_GROUP:].astype(BF16)


def _proj(x, w_in, cs):
    n = x.shape[0]
    tm = PROJ_TM
    half = jax.ShapeDtypeStruct((n, D_ATTN), BF16)
    vmem = (D_MODEL * D_IN * 2 + 2 * tm * D_MODEL * 4 + 10 * tm * D_ATTN * 2
            + 4 * tm * D_IN * 4)
    out_spec = pl.BlockSpec((tm, D_ATTN), lambda i: (i, 0))
    return pl.pallas_call(
        _proj_kernel,
        grid=(n // tm,),
        in_specs=[
            pl.BlockSpec((tm, D_MODEL), lambda i: (i, 0)),
            _const_spec((D_MODEL, D_IN)),
            _const_spec((FOURIER_GROUP, 2 * FOURIER_GROUP)),
        ],
        out_specs=[out_spec] * 5,
        out_shape=[half] * 5,
        compiler_params=_cparams(vmem, 1),
        name="proj",
    )(x, w_in, cs)


PAIR_Q = 2 * GRID_W
PAIR_KROWS = WIN_ROWS + 2
PAIR_K = PAIR_KROWS * GRID_W
ATT_PAIRS_PER_STEP = 8
N_BIAS_TYPES = 5


def _pair_geometry(p, n_rows):
    w = min(max(2 * p - WIN_ROWS // 2, 0), n_rows - PAIR_KROWS)
    r0 = [min(max(2 * p + a - WIN_ROWS // 2, 0), n_rows - WIN_ROWS) for a in (0, 1)]
    return w, r0


def _bias_tables(rpb_l, n_rows):
    n_pairs = n_rows // 2
    cols = np.arange(GRID_W)
    c0 = np.clip(cols - WIN_COLS // 2, 0, GRID_W - WIN_COLS)
    in_win = (cols[None, :] >= c0[:, None]) & (cols[None, :] < c0[:, None] + WIN_COLS)
    dc_idx = np.clip(cols[None, :] - cols[:, None], -(WIN_COLS - 1), WIN_COLS - 1) + (WIN_COLS - 1)
    dr_idx = np.zeros((N_BIAS_TYPES, 2, PAIR_KROWS), np.int32)
    row_ok = np.zeros((N_BIAS_TYPES, 2, PAIR_KROWS), bool)
    for t, p in enumerate((0, 1, 2, n_pairs - 2, n_pairs - 1)):
        w, r0 = _pair_geometry(p, n_rows)
        for a in (0, 1):
            kr = w + np.arange(PAIR_KROWS)
            row_ok[t, a] = (kr >= r0[a]) & (kr < r0[a] + WIN_ROWS)
            dr_idx[t, a] = np.clip(kr - (2 * p + a) + (WIN_ROWS - 1), 0, 2 * WIN_ROWS - 2)
    vals = rpb_l.astype(F32)[:, dr_idx[:, :, None, :, None], dc_idx[None, None, :, None, :]]
    ok = row_ok[:, :, None, :, None] & in_win[None, None, :, None, :]
    bias = jnp.where(ok[None], vals, NEG_INF)
    bias = bias.transpose(1, 0, 2, 3, 4, 5)
    return bias.reshape(N_BIAS_TYPES, N_ATTN_HEADS, PAIR_Q, PAIR_K)


def _nattn_kernel(q_ref, k_ref, v_ref, bias_ref, o_ref, *, n_rows):
    step = pl.program_id(1)
    n_pairs = n_rows // 2
    lane = lax.broadcasted_iota(jnp.int32, (PAIR_Q, V7X_LANES), 1)
    first_head = lane < HEAD_DIM

    def pair_body(pl_idx, carry):
        p = step * ATT_PAIRS_PER_STEP + pl_idx
        w = jnp.clip(2 * p - WIN_ROWS // 2, 0, n_rows - PAIR_KROWS)
        kstart = pl.multiple_of(w * GRID_W, V7X_LANES)
        typ = jnp.where(p <= 1, p, jnp.where(p >= n_pairs - 2, p - (n_pairs - 5), 2))
        qrows = pl.ds(pl.multiple_of(pl_idx * PAIR_Q, PAIR_Q), PAIR_Q)
        for hp in range(N_ATTN_HEADS // 2):
            lanes = slice(hp * V7X_LANES, (hp + 1) * V7X_LANES)
            qp = q_ref[qrows, lanes]
            kp = k_ref[pl.ds(kstart, PAIR_K), lanes]
            vp = v_ref[pl.ds(kstart, PAIR_K), lanes]
            zero = jnp.zeros_like(qp)
            qq = jnp.concatenate([jnp.where(first_head, qp, zero),
                                  jnp.where(first_head, zero, qp)], axis=0)
            s = lax.dot_general(qq, kp, (((1,), (1,)), ((), ())),
                                preferred_element_type=F32)
            probs = []
            denom = []
            for e in (0, 1):
                se = s[e * PAIR_Q:(e + 1) * PAIR_Q] + bias_ref[typ, 2 * hp + e]
                m = jnp.max(se, axis=-1, keepdims=True)
                pe = jnp.exp(se - m)
                denom.append(jnp.sum(pe, axis=-1, keepdims=True))
                probs.append(pe.astype(BF16))
            o = _dot(jnp.concatenate(probs, axis=0), vp)
            o0 = o[:PAIR_Q] / denom[0]
            o1 = o[PAIR_Q:] / denom[1]
            o_ref[qrows, lanes] = jnp.where(first_head, o0, o1)
        return carry

    lax.fori_loop(0, ATT_PAIRS_PER_STEP, pair_body, 0)


def _nattn(q, k, v, bias, batch, seq):
    n_rows = seq // GRID_W
    tq = ATT_PAIRS_PER_STEP * PAIR_Q
    steps = seq // tq
    k3 = k.reshape(batch, seq, D_ATTN)
    v3 = v.reshape(batch, seq, D_ATTN)
    vmem = (2 * 2 * seq * D_ATTN * 2
            + N_BIAS_TYPES * N_ATTN_HEADS * PAIR_Q * PAIR_K * 4
            + 2 * tq * D_ATTN * (2 + 4)
            + (6 << 20))
    kv_spec = pl.BlockSpec((None, seq, D_ATTN), lambda b, i: (b, 0, 0))
    return pl.pallas_call(
        functools.partial(_nattn_kernel, n_rows=n_rows),
        grid=(batch, steps),
        in_specs=[
            pl.BlockSpec((tq, D_ATTN), lambda b, i: (b * steps + i, 0)),
            kv_spec,
            kv_spec,
            _const_spec((N_BIAS_TYPES, N_ATTN_HEADS, PAIR_Q, PAIR_K)),
        ],
        out_specs=pl.BlockSpec((tq, D_ATTN), lambda b, i: (b * steps + i, 0)),
        out_shape=jax.ShapeDtypeStruct((batch * seq, D_ATTN), F32),
        compiler_params=_cparams(vmem, 2),
        name="nattn",
    )(q, k3, v3, bias)


SEQ_N1 = 128
SEQ_N2 = 64
FN1_S2_PER_STEP = 8
FN2_K1_PER_STEP = 8


def _fnet1_kernel(a_ref, b_ref, w1_ref, tc_ref, ts_ref, pr_ref, pi_ref):
    x = jnp.concatenate([a_ref[...], b_ref[...]], axis=0)
    t = _dot(w1_ref[...], x)
    tr = t[:SEQ_N1]
    ti = t[SEQ_N1:]
    tcos = tc_ref[...]
    tsin = ts_ref[...]
    for j in range(FN1_S2_PER_STEP):
        lanes = slice(j * D_FOURIER, (j + 1) * D_FOURIER)
        c = tcos[:, j:j + 1]
        s = tsin[:, j:j + 1]
        pr_ref[:, lanes] = (tr[:, lanes] * c + ti[:, lanes] * s).astype(BF16)
        pi_ref[:, lanes] = (ti[:, lanes] * c - tr[:, lanes] * s).astype(BF16)


def _fnet1(a, b, w1, tcos, tsin, batch, seq):
    lanes_total = SEQ_N2 * D_FOURIER
    a3 = a.reshape(batch, SEQ_N1, lanes_total)
    b3 = b.reshape(batch, SEQ_N1, lanes_total)
    blk = FN1_S2_PER_STEP * D_FOURIER
    nblk = SEQ_N2 // FN1_S2_PER_STEP
    io_spec = pl.BlockSpec((None, SEQ_N1, blk), lambda j, bb: (bb, 0, j))
    tw_spec = pl.BlockSpec((None, SEQ_N1, FN1_S2_PER_STEP), lambda j, bb: (j, 0, 0))
    shape = jax.ShapeDtypeStruct((batch, SEQ_N1, lanes_total), BF16)
    vmem = 8 * SEQ_N1 * blk * 2 + 8 * 2 * SEQ_N1 * blk * 4 + (4 << 20)
    return pl.pallas_call(
        _fnet1_kernel,
        grid=(nblk, batch),
        in_specs=[io_spec, io_spec, _const_spec((2 * SEQ_N1, 2 * SEQ_N1)), tw_spec, tw_spec],
        out_specs=[io_spec, io_spec],
        out_shape=[shape, shape],
        compiler_params=_cparams(vmem, 2),
        name="fnet1",
    )(a3, b3, w1, tcos, tsin)


def _fnet2_kernel(pr_ref, pi_ref, w2_ref, y_ref):
    for j in range(FN2_K1_PER_STEP):
        rows = slice(j * SEQ_N2, (j + 1) * SEQ_N2)
        x = jnp.concatenate([pr_ref[rows, :], pi_ref[rows, :]], axis=0)
        y_ref[:, j * D_FOURIER:(j + 1) * D_FOURIER] = _dot(w2_ref[...], x)


def _fnet2(pr, pi, w2, batch, seq):
    pr3 = pr.reshape(batch, seq, D_FOURIER)
    pi3 = pi.reshape(batch, seq, D_FOURIER)
    rows = FN2_K1_PER_STEP * SEQ_N2
    blk = FN2_K1_PER_STEP * D_FOURIER
    in_spec = pl.BlockSpec((None, rows, D_FOURIER), lambda bb, j: (bb, j, 0))
    vmem = 4 * rows * D_FOURIER * 2 + 2 * SEQ_N2 * blk * 4 + (4 << 20)
    y = pl.pallas_call(
        _fnet2_kernel,
        grid=(batch, SEQ_N1 // FN2_K1_PER_STEP),
        in_specs=[in_spec, in_spec, _const_spec((SEQ_N2, 2 * SEQ_N2))],
        out_specs=pl.BlockSpec((None, SEQ_N2, blk), lambda bb, j: (bb, 0, j)),
        out_shape=jax.ShapeDtypeStruct((batch, SEQ_N2, SEQ_N1 * D_FOURIER), F32),
        compiler_params=_cparams(vmem, 2),
        name="fnet2",
    )(pr3, pi3, w2)
    return y.reshape(batch * seq, D_FOURIER)


def _dft_constants():
    def cs(n):
        idx = np.arange(n)
        ang = 2.0 * np.pi * ((idx[:, None] * idx[None, :]) % n) / n
        return np.cos(ang), np.sin(ang)

    cc, sc = cs(FOURIER_GROUP)
    chan = np.concatenate([cc, sc], axis=1)
    c1, s1 = cs(SEQ_N1)
    w1 = np.block([[c1, -s1], [-s1, -c1]])
    c2, s2 = cs(SEQ_N2)
    norm = 1.0 / math.sqrt(SEQ_N1 * SEQ_N2 * FOURIER_GROUP)
    w2 = np.concatenate([c2, s2], axis=1) * norm
    k1 = np.arange(SEQ_N1)
    s2i = np.arange(SEQ_N2)
    ang = 2.0 * np.pi * (k1[:, None] * s2i[None, :]) / (SEQ_N1 * SEQ_N2)
    nblk = SEQ_N2 // FN1_S2_PER_STEP

    def blocked(t):
        return t.reshape(SEQ_N1, nblk, FN1_S2_PER_STEP).transpose(1, 0, 2)

    return (jnp.asarray(chan, F32).astype(BF16), jnp.asarray(w1, F32).astype(BF16),
            jnp.asarray(w2, F32).astype(BF16),
            jnp.asarray(blocked(np.cos(ang)), F32), jnp.asarray(blocked(np.sin(ang)), F32))


MERGE_TM = 512


def _merge_ln_kernel(attn_ref, four_ref, x_ref, ga_ref, gf_ref, wo_ref, g_ref, b_ref, o_ref):
    ma = _rms_norm(attn_ref[...], ga_ref[...]).astype(BF16)
    mf = _rms_norm(four_ref[...], gf_ref[...]).astype(BF16)
    z = _dot(ma, wo_ref[0:D_ATTN, :]) + _dot(mf, wo_ref[D_ATTN:, :])
    o_ref[...] = _layer_norm(ALPHA * x_ref[...] + z, g_ref[...], b_ref[...])


def _merge_ln(attn, four, x, ga, gf, wo, g, b):
    n = x.shape[0]
    tm = MERGE_TM
    vmem = (D_MODEL * D_MODEL * 2 + 4 * tm * D_ATTN * 4 + 4 * tm * D_MODEL * 4
            + 6 * tm * D_MODEL * 4)
    return pl.pallas_call(
        _merge_ln_kernel,
        grid=(n // tm,),
        in_specs=[
            pl.BlockSpec((tm, D_ATTN), lambda i: (i, 0)),
            pl.BlockSpec((tm, D_FOURIER), lambda i: (i, 0)),
            pl.BlockSpec((tm, D_MODEL), lambda i: (i, 0)),
            _const_spec((1, D_ATTN)),
            _const_spec((1, D_FOURIER)),
            _const_spec((D_MODEL, D_MODEL)),
            _const_spec((1, D_MODEL)),
            _const_spec((1, D_MODEL)),
        ],
        out_specs=pl.BlockSpec((tm, D_MODEL), lambda i: (i, 0)),
        out_shape=jax.ShapeDtypeStruct((n, D_MODEL), F32),
        compiler_params=_cparams(vmem, 1),
        name="merge_ln",
    )(attn, four, x, ga, gf, wo, g, b)


def kernel(x, ffn1_w_gate, ffn1_w_up, ffn1_w_down, ln1_g, ln1_b, w_in, rpb, g_attn, g_fourier,
           w_out, ln2_g, ln2_b, ffn2_w_gate, ffn2_w_up, ffn2_w_down, ln3_g, ln3_b):
    batch, seq, d = x.shape
    assert d == D_MODEL and seq == SEQ_N1 * SEQ_N2 and seq % (ATT_PAIRS_PER_STEP * PAIR_Q) == 0
    assert rpb.shape[0] == DEPTH
    n_rows = seq // GRID_W
    chan, w1, w2, tcos, tsin = _dft_constants()

    def row(p, l):
        return p[l].reshape(1, -1).astype(F32)

    h = x.reshape(batch * seq, d).astype(F32)
    for l in range(DEPTH):
        h = _ffn_ln(h, ffn1_w_gate[l].astype(BF16), ffn1_w_up[l].astype(BF16),
                    ffn1_w_down[l].astype(BF16), row(ln1_g, l), row(ln1_b, l))
        q, k, v, a, b = _proj(h, w_in[l].astype(BF16), chan)
        attn = _nattn(q, k, v, _bias_tables(rpb[l], n_rows), batch, seq)
        pr, pi = _fnet1(a, b, w1, tcos, tsin, batch, seq)
        four = _fnet2(pr, pi, w2, batch, seq)
        h = _merge_ln(attn, four, h, row(g_attn, l), row(g_fourier, l), w_out[l].astype(BF16),
                      row(ln2_g, l), row(ln2_b, l))
        h = _ffn_ln(h, ffn2_w_gate[l].astype(BF16), ffn2_w_up[l].astype(BF16),
                    ffn2_w_down[l].astype(BF16), row(ln3_g, l), row(ln3_b, l))
    return h.reshape(batch, seq, d).astype(x.dtype)
```

```python
import functools
import math

import jax
import jax.numpy as jnp
import numpy as np
from jax import lax
from jax.experimental import pallas as pl
from jax.experimental.pallas import tpu as pltpu

D_MODEL = 1024
DEPTH = 2
GRID_W = 64
WIN_ROWS = 8
WIN_COLS = 16
N_ATTN_HEADS = 8
HEAD_DIM = 64
D_ATTN = N_ATTN_HEADS * HEAD_DIM
N_FOURIER_GROUPS = 4
FOURIER_GROUP = 128
D_FOURIER = N_FOURIER_GROUPS * FOURIER_GROUP
D_IN = 3 * D_ATTN + D_FOURIER
D_FF = 2816
ALPHA = (2.0 * DEPTH) ** 0.25
LN_EPS = 1e-5
RMS_EPS = 1e-6
NEG_INF = -1e30

V7X_LANES = 128
V7X_MXU_DIM = 256
V7X_VMEM_BYTES = 64 * 1024 * 1024

F32 = jnp.float32
BF16 = jnp.bfloat16


def _cparams(vmem_bytes, n_axes):
    return pltpu.CompilerParams(
        dimension_semantics=("arbitrary",) * n_axes,
        vmem_limit_bytes=min(int(vmem_bytes), V7X_VMEM_BYTES - (2 << 20)),
    )


def _const_spec(shape):
    nd = len(shape)
    return pl.BlockSpec(shape, lambda *_: (0,) * nd, pipeline_mode=pl.Buffered(1))


def _layer_norm(z, g, b):
    mu = jnp.mean(z, axis=-1, keepdims=True)
    zc = z - mu
    var = jnp.mean(zc * zc, axis=-1, keepdims=True)
    return zc * lax.rsqrt(var + LN_EPS) * g + b


def _rms_norm(z, g):
    return z * lax.rsqrt(jnp.mean(z * z, axis=-1, keepdims=True) + RMS_EPS) * g


def _dot(a, b):
    return jnp.dot(a, b, preferred_element_type=F32)


FFN_TM = 512
FFN_FC = V7X_MXU_DIM


def _ffn_ln_kernel(x_ref, wg_ref, wu_ref, wd_ref, g_ref, b_ref, o_ref, h_ref):
    x = x_ref[...]
    xb = x.astype(BF16)
    for c in range(D_FF // FFN_FC):
        cols = slice(c * FFN_FC, (c + 1) * FFN_FC)
        gate = _dot(xb, wg_ref[:, cols])
        up = _dot(xb, wu_ref[:, cols])
        h_ref[:, cols] = (gate * jax.nn.sigmoid(gate) * up).astype(BF16)
    y = _dot(h_ref[...], wd_ref[...])
    o_ref[...] = _layer_norm(ALPHA * x + 0.5 * y, g_ref[...], b_ref[...])


def _ffn_ln(x, wg, wu, wd, g, b):
    n = x.shape[0]
    tm = FFN_TM
    vmem = (3 * D_MODEL * D_FF * 2
            + 4 * tm * D_MODEL * 4
            + tm * D_FF * 2
            + 6 * tm * D_MODEL * 4)
    return pl.pallas_call(
        _ffn_ln_kernel,
        grid=(n // tm,),
        in_specs=[
            pl.BlockSpec((tm, D_MODEL), lambda i: (i, 0)),
            _const_spec((D_MODEL, D_FF)),
            _const_spec((D_MODEL, D_FF)),
            _const_spec((D_FF, D_MODEL)),
            _const_spec((1, D_MODEL)),
            _const_spec((1, D_MODEL)),
        ],
        out_specs=pl.BlockSpec((tm, D_MODEL), lambda i: (i, 0)),
        out_shape=jax.ShapeDtypeStruct((n, D_MODEL), F32),
        scratch_shapes=[pltpu.VMEM((tm, D_FF), BF16)],
        compiler_params=_cparams(vmem, 1),
        name="ffn_ln",
    )(x, wg, wu, wd, g, b)


PROJ_TM = 512


def _proj_kernel(x_ref, w_ref, cs_ref, q_ref, k_ref, v_ref, a_ref, b_ref):
    xb = x_ref[...].astype(BF16)
    q_ref[...] = (_dot(xb, w_ref[:, 0:D_ATTN]) * (HEAD_DIM ** -0.5)).astype(BF16)
    k_ref[...] = _dot(xb, w_ref[:, D_ATTN:2 * D_ATTN]).astype(BF16)
    v_ref[...] = _dot(xb, w_ref[:, 2 * D_ATTN:3 * D_ATTN]).astype(BF16)
    u = _dot(xb, w_ref[:, 3 * D_ATTN:]).astype(BF16)
    for grp in range(N_FOURIER_GROUPS):
        cols = slice(grp * FOURIER_GROUP, (grp + 1) * FOURIER_GROUP)
        ab = _dot(u[:, cols], cs_ref[...])
        a_ref[grp] = ab[:, :FOURIER_GROUP]
        b_ref[grp] = ab[:, FOURIER_GROUP:]


def _proj(x, w_in, cs):
    n = x.shape[0]
    tm = PROJ_TM
    half = jax.ShapeDtypeStruct((n, D_ATTN), BF16)
    four = jax.ShapeDtypeStruct((N_FOURIER_GROUPS, n, FOURIER_GROUP), F32)
    vmem = (D_MODEL * D_IN * 2 + 2 * tm * D_MODEL * 4 + 6 * tm * D_ATTN * 2
            + 4 * tm * D_FOURIER * 4 + 4 * tm * D_IN * 4)
    out_spec = pl.BlockSpec((tm, D_ATTN), lambda i: (i, 0))
    four_spec = pl.BlockSpec((N_FOURIER_GROUPS, tm, FOURIER_GROUP), lambda i: (0, i, 0))
    return pl.pallas_call(
        _proj_kernel,
        grid=(n // tm,),
        in_specs=[
            pl.BlockSpec((tm, D_MODEL), lambda i: (i, 0)),
            _const_spec((D_MODEL, D_IN)),
            _const_spec((FOURIER_GROUP, 2 * FOURIER_GROUP)),
        ],
        out_specs=[out_spec] * 3 + [four_spec] * 2,
        out_shape=[half] * 3 + [four] * 2,
        compiler_params=_cparams(vmem, 1),
        name="proj",
    )(x, w_in, cs)


PAIR_Q = 2 * GRID_W
PAIR_KROWS = WIN_ROWS + 2
PAIR_K = PAIR_KROWS * GRID_W
ATT_PAIRS_PER_STEP = 8
N_BIAS_TYPES = 5


def _pair_geometry(p, n_rows):
    w = min(max(2 * p - WIN_ROWS // 2, 0), n_rows - PAIR_KROWS)
    r0 = [min(max(2 * p + a - WIN_ROWS // 2, 0), n_rows - WIN_ROWS) for a in (0, 1)]
    return w, r0


def _toeplitz_tables(rpb_l):
    cols = np.arange(GRID_W)
    c0 = np.clip(cols - WIN_COLS // 2, 0, GRID_W - WIN_COLS)
    in_win = (cols[None, :] >= c0[:, None]) & (cols[None, :] < c0[:, None] + WIN_COLS)
    dc_idx = np.clip(cols[None, :] - cols[:, None], -(WIN_COLS - 1), WIN_COLS - 1) + (WIN_COLS - 1)
    onehot = (dc_idx[None] == np.arange(2 * WIN_COLS - 1)[:, None, None]).astype(np.float32)
    tz = jnp.einsum("hdj,jqk->hdqk", rpb_l.astype(F32), jnp.asarray(onehot),
                    precision=lax.Precision.HIGHEST)
    tz = jnp.where(in_win, tz, NEG_INF)
    tzp = jnp.pad(tz, ((0, 0), (1, 1), (0, 0), (0, 0)), constant_values=NEG_INF)
    return jnp.concatenate([tzp[:, :-1], tzp[:, 1:]], axis=-1)


def _row_masks(n_rows):
    n_pairs = n_rows // 2
    out = np.zeros((N_BIAS_TYPES, 2, PAIR_KROWS, GRID_W), np.float32)
    for t, p in enumerate((0, 1, 2, n_pairs - 2, n_pairs - 1)):
        w, r0 = _pair_geometry(p, n_rows)
        kr = w + np.arange(PAIR_KROWS)
        for a in (0, 1):
            ok = (kr >= r0[a]) & (kr < r0[a] + WIN_ROWS)
            out[t, a] = np.where(ok, 0.0, NEG_INF)[:, None]
    return jnp.asarray(out.reshape(2 * N_BIAS_TYPES, PAIR_K))


def _nattn_kernel(q_ref, k_ref, v_ref, tz_ref, rm_ref, o_ref, *, n_rows):
    step = pl.program_id(1)
    n_pairs = n_rows // 2
    lane = lax.broadcasted_iota(jnp.int32, (PAIR_Q, V7X_LANES), 1)
    first_head = lane < HEAD_DIM
    n_tz = 2 * WIN_ROWS

    def pair_body(pl_idx, carry):
        p = step * ATT_PAIRS_PER_STEP + pl_idx
        w = jnp.clip(2 * p - WIN_ROWS // 2, 0, n_rows - PAIR_KROWS)
        kstart = pl.multiple_of(w * GRID_W, V7X_LANES)
        typ = jnp.where(p <= 1, p, jnp.where(p >= n_pairs - 2, p - (n_pairs - 5), 2))
        qrows = pl.ds(pl.multiple_of(pl_idx * PAIR_Q, PAIR_Q), PAIR_Q)
        tz_idx = [[jnp.clip(w + 2 * t - 2 * p - a + WIN_ROWS, 0, n_tz - 1)
                   for t in range(PAIR_KROWS // 2)] for a in (0, 1)]
        row_mask = [rm_ref[pl.ds(2 * typ + a, 1), :] for a in (0, 1)]
        for hp in range(N_ATTN_HEADS // 2):
            lanes = slice(hp * V7X_LANES, (hp + 1) * V7X_LANES)
            qp = q_ref[qrows, lanes]
            kp = k_ref[pl.ds(kstart, PAIR_K), lanes]
            vp = v_ref[pl.ds(kstart, PAIR_K), lanes]
            zero = jnp.zeros_like(qp)
            qq = jnp.concatenate([jnp.where(first_head, qp, zero),
                                  jnp.where(first_head, zero, qp)], axis=0)
            s = lax.dot_general(qq, kp, (((1,), (1,)), ((), ())),
                                preferred_element_type=F32)
            probs = []
            denom = []
            for e in (0, 1):
                h = 2 * hp + e
                bias = jnp.concatenate(
                    [jnp.concatenate([tz_ref[h, tz_idx[a][t]] for t in range(PAIR_KROWS // 2)],
                                     axis=1) + row_mask[a] for a in (0, 1)], axis=0)
                se = s[e * PAIR_Q:(e + 1) * PAIR_Q] + bias
                m = jnp.max(se, axis=-1, keepdims=True)
                pe = jnp.exp(se - m)
                denom.append(jnp.sum(pe, axis=-1, keepdims=True))
                probs.append(pe.astype(BF16))
            o = _dot(jnp.concatenate(probs, axis=0), vp)
            o0 = o[:PAIR_Q] / denom[0]
            o1 = o[PAIR_Q:] / denom[1]
            o_ref[qrows, lanes] = jnp.where(first_head, o0, o1)
        return carry

    lax.fori_loop(0, ATT_PAIRS_PER_STEP, pair_body, 0)


def _nattn(q, k, v, tz2, row_masks, batch, seq):
    n_rows = seq // GRID_W
    tq = ATT_PAIRS_PER_STEP * PAIR_Q
    steps = seq // tq
    k3 = k.reshape(batch, seq, D_ATTN)
    v3 = v.reshape(batch, seq, D_ATTN)
    vmem = (2 * 2 * seq * D_ATTN * 2
            + tz2.size * 4
            + 2 * tq * D_ATTN * (2 + 4)
            + (8 << 20))
    kv_spec = pl.BlockSpec((None, seq, D_ATTN), lambda b, i: (b, 0, 0))
    return pl.pallas_call(
        functools.partial(_nattn_kernel, n_rows=n_rows),
        grid=(batch, steps),
        in_specs=[
            pl.BlockSpec((tq, D_ATTN), lambda b, i: (b * steps + i, 0)),
            kv_spec,
            kv_spec,
            _const_spec(tz2.shape),
            _const_spec(row_masks.shape),
        ],
        out_specs=pl.BlockSpec((tq, D_ATTN), lambda b, i: (b * steps + i, 0)),
        out_shape=jax.ShapeDtypeStruct((batch * seq, D_ATTN), F32),
        compiler_params=_cparams(vmem, 2),
        name="nattn",
    )(q, k3, v3, tz2, row_masks)


SEQ_N1 = 128
SEQ_N2 = 64
FN1_S2_PER_STEP = 8
FN2_K1_PER_STEP = 8


FN1_ROWS = SEQ_N1 * FN1_S2_PER_STEP


def _fnet1_kernel(a_ref, b_ref, w1_ref, tc_ref, ts_ref, pr_ref, pi_ref):
    flat = (N_FOURIER_GROUPS * FN1_ROWS, FOURIER_GROUP)
    a2, b2 = a_ref.reshape(*flat), b_ref.reshape(*flat)
    pr2, pi2 = pr_ref.reshape(*flat), pi_ref.reshape(*flat)
    tcos = tc_ref[...]
    tsin = ts_ref[...]

    def rows(grp, j):
        return pl.ds(grp * FN1_ROWS + j, SEQ_N1, stride=FN1_S2_PER_STEP)

    def gather(ref2, j):
        return jnp.concatenate([ref2[rows(grp, j), :] for grp in range(N_FOURIER_GROUPS)], axis=1)

    for j in range(FN1_S2_PER_STEP):
        x = jnp.concatenate([gather(a2, j).astype(BF16), gather(b2, j).astype(BF16)],
                            axis=0)
        t = _dot(w1_ref[...], x)
        tr = t[:SEQ_N1]
        ti = t[SEQ_N1:]
        c = tcos[:, j:j + 1]
        s = tsin[:, j:j + 1]
        pr = tr * c + ti * s
        pi = ti * c - tr * s
        for grp in range(N_FOURIER_GROUPS):
            cols = slice(grp * FOURIER_GROUP, (grp + 1) * FOURIER_GROUP)
            pr2[rows(grp, j), :] = pr[:, cols]
            pi2[rows(grp, j), :] = pi[:, cols]


def _fnet1(a, b, w1, tcos, tsin, batch, seq):
    shape5 = (N_FOURIER_GROUPS, batch, SEQ_N1, SEQ_N2, FOURIER_GROUP)
    a5 = a.reshape(shape5)
    b5 = b.reshape(shape5)
    nblk = SEQ_N2 // FN1_S2_PER_STEP
    io_spec = pl.BlockSpec((N_FOURIER_GROUPS, None, SEQ_N1, FN1_S2_PER_STEP, FOURIER_GROUP),
                           lambda bb, j: (0, bb, 0, j, 0))
    tw_spec = pl.BlockSpec((None, SEQ_N1, FN1_S2_PER_STEP), lambda bb, j: (j, 0, 0))
    shape = jax.ShapeDtypeStruct(shape5, F32)
    blk_bytes = N_FOURIER_GROUPS * FN1_ROWS * FOURIER_GROUP * 4
    vmem = 8 * blk_bytes + (8 << 20)
    return pl.pallas_call(
        _fnet1_kernel,
        grid=(batch, nblk),
        in_specs=[io_spec, io_spec, _const_spec((2 * SEQ_N1, 2 * SEQ_N1)), tw_spec, tw_spec],
        out_specs=[io_spec, io_spec],
        out_shape=[shape, shape],
        compiler_params=_cparams(vmem, 2),
        name="fnet1",
    )(a5, b5, w1, tcos, tsin)


def _fnet2_kernel(pr_ref, pi_ref, w2_ref, y_ref):
    def slab(ref, j):
        rows = slice(j * SEQ_N2, (j + 1) * SEQ_N2)
        return jnp.concatenate([ref[grp, rows, :] for grp in range(N_FOURIER_GROUPS)],
                               axis=1).astype(BF16)

    for j in range(FN2_K1_PER_STEP):
        x = jnp.concatenate([slab(pr_ref, j), slab(pi_ref, j)], axis=0)
        y_ref[:, j, :] = _dot(w2_ref[...], x)


def _fnet2(pr, pi, w2, batch, seq):
    shape4 = (N_FOURIER_GROUPS, batch, seq, FOURIER_GROUP)
    pr3 = pr.reshape(shape4)
    pi3 = pi.reshape(shape4)
    rows = FN2_K1_PER_STEP * SEQ_N2
    in_spec = pl.BlockSpec((N_FOURIER_GROUPS, None, rows, FOURIER_GROUP),
                           lambda bb, j: (0, bb, j, 0))
    vmem = 4 * rows * D_FOURIER * 4 + 2 * SEQ_N2 * FN2_K1_PER_STEP * D_FOURIER * 4 + (8 << 20)
    y = pl.pallas_call(
        _fnet2_kernel,
        grid=(batch, SEQ_N1 // FN2_K1_PER_STEP),
        in_specs=[in_spec, in_spec, _const_spec((SEQ_N2, 2 * SEQ_N2))],
        out_specs=pl.BlockSpec((None, SEQ_N2, FN2_K1_PER_STEP, D_FOURIER),
                               lambda bb, j: (bb, 0, j, 0)),
        out_shape=jax.ShapeDtypeStruct((batch, SEQ_N2, SEQ_N1, D_FOURIER), F32),
        compiler_params=_cparams(vmem, 2),
        name="fnet2",
    )(pr3, pi3, w2)
    return y.reshape(batch * seq, D_FOURIER)


def _dft_constants():
    def cs(n):
        idx = np.arange(n)
        ang = 2.0 * np.pi * ((idx[:, None] * idx[None, :]) % n) / n
        return np.cos(ang), np.sin(ang)

    cc, sc = cs(FOURIER_GROUP)
    chan = np.concatenate([cc, sc], axis=1)
    c1, s1 = cs(SEQ_N1)
    w1 = np.block([[c1, -s1], [-s1, -c1]])
    c2, s2 = cs(SEQ_N2)
    norm = 1.0 / math.sqrt(SEQ_N1 * SEQ_N2 * FOURIER_GROUP)
    w2 = np.concatenate([c2, s2], axis=1) * norm
    k1 = np.arange(SEQ_N1)
    s2i = np.arange(SEQ_N2)
    ang = 2.0 * np.pi * (k1[:, None] * s2i[None, :]) / (SEQ_N1 * SEQ_N2)
    nblk = SEQ_N2 // FN1_S2_PER_STEP

    def blocked(t):
        return t.reshape(SEQ_N1, nblk, FN1_S2_PER_STEP).transpose(1, 0, 2)

    return (jnp.asarray(chan, F32).astype(BF16), jnp.asarray(w1, F32).astype(BF16),
            jnp.asarray(w2, F32).astype(BF16),
            jnp.asarray(blocked(np.cos(ang)), F32), jnp.asarray(blocked(np.sin(ang)), F32))


MERGE_TM = 512


def _merge_ln_kernel(attn_ref, four_ref, x_ref, ga_ref, gf_ref, wo_ref, g_ref, b_ref, o_ref):
    ma = _rms_norm(attn_ref[...], ga_ref[...]).astype(BF16)
    mf = _rms_norm(four_ref[...], gf_ref[...]).astype(BF16)
    z = _dot(ma, wo_ref[0:D_ATTN, :]) + _dot(mf, wo_ref[D_ATTN:, :])
    o_ref[...] = _layer_norm(ALPHA * x_ref[...] + z, g_ref[...], b_ref[...])


def _merge_ln(attn, four, x, ga, gf, wo, g, b):
    n = x.shape[0]
    tm = MERGE_TM
    vmem = (D_MODEL * D_MODEL * 2 + 4 * tm * D_ATTN * 4 + 4 * tm * D_MODEL * 4
            + 6 * tm * D_MODEL * 4)
    return pl.pallas_call(
        _merge_ln_kernel,
        grid=(n // tm,),
        in_specs=[
            pl.BlockSpec((tm, D_ATTN), lambda i: (i, 0)),
            pl.BlockSpec((tm, D_FOURIER), lambda i: (i, 0)),
            pl.BlockSpec((tm, D_MODEL), lambda i: (i, 0)),
            _const_spec((1, D_ATTN)),
            _const_spec((1, D_FOURIER)),
            _const_spec((D_MODEL, D_MODEL)),
            _const_spec((1, D_MODEL)),
            _const_spec((1, D_MODEL)),
        ],
        out_specs=pl.BlockSpec((tm, D_MODEL), lambda i: (i, 0)),
        out_shape=jax.ShapeDtypeStruct((n, D_MODEL), F32),
        compiler_params=_cparams(vmem, 1),
        name="merge_ln",
    )(attn, four, x, ga, gf, wo, g, b)


def kernel(x, ffn1_w_gate, ffn1_w_up, ffn1_w_down, ln1_g, ln1_b, w_in, rpb, g_attn, g_fourier,
           w_out, ln2_g, ln2_b, ffn2_w_gate, ffn2_w_up, ffn2_w_down, ln3_g, ln3_b):
    batch, seq, d = x.shape
    assert d == D_MODEL and seq == SEQ_N1 * SEQ_N2 and seq % (ATT_PAIRS_PER_STEP * PAIR_Q) == 0
    assert rpb.shape[0] == DEPTH
    n_rows = seq // GRID_W
    chan, w1, w2, tcos, tsin = _dft_constants()
    row_masks = _row_masks(n_rows)

    def row(p, l):
        return p[l].reshape(1, -1).astype(F32)

    h = x.reshape(batch * seq, d).astype(F32)
    for l in range(DEPTH):
        h = _ffn_ln(h, ffn1_w_gate[l].astype(BF16), ffn1_w_up[l].astype(BF16),
                    ffn1_w_down[l].astype(BF16), row(ln1_g, l), row(ln1_b, l))
        q, k, v, a, b = _proj(h, w_in[l].astype(BF16), chan)
        attn = _nattn(q, k, v, _toeplitz_tables(rpb[l]), row_masks, batch, seq)
        pr, pi = _fnet1(a, b, w1, tcos, tsin, batch, seq)
        four = _fnet2(pr, pi, w2, batch, seq)
        h = _merge_ln(attn, four, h, row(g_attn, l), row(g_fourier, l), w_out[l].astype(BF16),
                      row(ln2_g, l), row(ln2_b, l))
        h = _ffn_ln(h, ffn2_w_gate[l].astype(BF16), ffn2_w_up[l].astype(BF16),
                    ffn2_w_down[l].astype(BF16), row(ln3_g, l), row(ln3_b, l))
    return h.reshape(batch, seq, d).astype(x.dtype)
```

```python
import functools
import math

import jax
import jax.numpy as jnp
import numpy as np
from jax import lax
from jax.experimental import pallas as pl
from jax.experimental.pallas import tpu as pltpu

D_MODEL = 1024
DEPTH = 2
GRID_W = 64
WIN_ROWS = 8
WIN_COLS = 16
N_ATTN_HEADS = 8
HEAD_DIM = 64
D_ATTN = N_ATTN_HEADS * HEAD_DIM
N_FOURIER_GROUPS = 4
FOURIER_GROUP = 128
D_FOURIER = N_FOURIER_GROUPS * FOURIER_GROUP
D_IN = 3 * D_ATTN + D_FOURIER
D_FF = 2816
ALPHA = (2.0 * DEPTH) ** 0.25
LN_EPS = 1e-5
RMS_EPS = 1e-6
NEG_INF = -1e30

V7X_LANES = 128
V7X_MXU_DIM = 256
V7X_VMEM_BYTES = 64 * 1024 * 1024

F32 = jnp.float32
BF16 = jnp.bfloat16


def _cparams(vmem_bytes, n_axes):
    return pltpu.CompilerParams(
        dimension_semantics=("arbitrary",) * n_axes,
        vmem_limit_bytes=min(int(vmem_bytes), V7X_VMEM_BYTES - (2 << 20)),
    )


def _const_spec(shape):
    nd = len(shape)
    return pl.BlockSpec(shape, lambda *_: (0,) * nd, pipeline_mode=pl.Buffered(1))


def _layer_norm(z, g, b):
    mu = jnp.mean(z, axis=-1, keepdims=True)
    zc = z - mu
    var = jnp.mean(zc * zc, axis=-1, keepdims=True)
    return zc * lax.rsqrt(var + LN_EPS) * g + b


def _rms_norm(z, g):
    return z * lax.rsqrt(jnp.mean(z * z, axis=-1, keepdims=True) + RMS_EPS) * g


def _dot(a, b):
    return jnp.dot(a, b, preferred_element_type=F32)


FFN_TM = 512
FFN_SUB = 512
FFN_FC = V7X_MXU_DIM


def _ffn_ln_kernel(x_ref, wg_ref, wu_ref, wd_ref, g_ref, b_ref, o_ref, h_ref):
    for sub in range(FFN_TM // FFN_SUB):
        rows = slice(sub * FFN_SUB, (sub + 1) * FFN_SUB)
        x = x_ref[rows, :]
        xb = x.astype(BF16)
        for c in range(D_FF // FFN_FC):
            cols = slice(c * FFN_FC, (c + 1) * FFN_FC)
            gate = _dot(xb, wg_ref[:, cols])
            up = _dot(xb, wu_ref[:, cols])
            h_ref[sub, :, cols] = (gate * jax.nn.sigmoid(gate) * up).astype(BF16)
        y = _dot(h_ref[sub], wd_ref[...])
        o_ref[rows, :] = _layer_norm(ALPHA * x + 0.5 * y, g_ref[...], b_ref[...])


def _ffn_ln(x, wg, wu, wd, g, b):
    n = x.shape[0]
    tm = FFN_TM
    vmem = (3 * D_MODEL * D_FF * 2
            + 4 * tm * D_MODEL * 4
            + tm * D_FF * 2
            + 6 * FFN_SUB * D_MODEL * 4)
    return pl.pallas_call(
        _ffn_ln_kernel,
        grid=(n // tm,),
        in_specs=[
            pl.BlockSpec((tm, D_MODEL), lambda i: (i, 0)),
            _const_spec((D_MODEL, D_FF)),
            _const_spec((D_MODEL, D_FF)),
            _const_spec((D_FF, D_MODEL)),
            _const_spec((1, D_MODEL)),
            _const_spec((1, D_MODEL)),
        ],
        out_specs=pl.BlockSpec((tm, D_MODEL), lambda i: (i, 0)),
        out_shape=jax.ShapeDtypeStruct((n, D_MODEL), F32),
        scratch_shapes=[pltpu.VMEM((tm // FFN_SUB, FFN_SUB, D_FF), BF16)],
        compiler_params=_cparams(vmem, 1),
        name="ffn_ln",
    )(x, wg, wu, wd, g, b)


PROJ_TM = 512
Q_SCALE = HEAD_DIM ** -0.5 * math.log2(math.e)


def _proj_kernel(x_ref, w_ref, cs_ref, q_ref, k_ref, v_ref, a_ref, b_ref):
    xb = x_ref[...].astype(BF16)
    q_ref[...] = (_dot(xb, w_ref[:, 0:D_ATTN]) * Q_SCALE).astype(BF16)
    k_ref[...] = _dot(xb, w_ref[:, D_ATTN:2 * D_ATTN]).astype(BF16)
    v_ref[...] = _dot(xb, w_ref[:, 2 * D_ATTN:3 * D_ATTN]).astype(BF16)
    u = _dot(xb, w_ref[:, 3 * D_ATTN:]).astype(BF16)
    for grp in range(N_FOURIER_GROUPS):
        cols = slice(grp * FOURIER_GROUP, (grp + 1) * FOURIER_GROUP)
        ab = _dot(u[:, cols], cs_ref[...])
        a_ref[grp] = ab[:, :FOURIER_GROUP]
        b_ref[grp] = ab[:, FOURIER_GROUP:]


def _proj(x, w_in, cs):
    n = x.shape[0]
    tm = PROJ_TM
    half = jax.ShapeDtypeStruct((n, D_ATTN), BF16)
    four = jax.ShapeDtypeStruct((N_FOURIER_GROUPS, n, FOURIER_GROUP), F32)
    vmem = (D_MODEL * D_IN * 2 + 2 * tm * D_MODEL * 4 + 6 * tm * D_ATTN * 2
            + 4 * tm * D_FOURIER * 4 + 4 * tm * D_IN * 4)
    out_spec = pl.BlockSpec((tm, D_ATTN), lambda i: (i, 0))
    four_spec = pl.BlockSpec((N_FOURIER_GROUPS, tm, FOURIER_GROUP), lambda i: (0, i, 0))
    return pl.pallas_call(
        _proj_kernel,
        grid=(n // tm,),
        in_specs=[
            pl.BlockSpec((tm, D_MODEL), lambda i: (i, 0)),
            _const_spec((D_MODEL, D_IN)),
            _const_spec((FOURIER_GROUP, 2 * FOURIER_GROUP)),
        ],
        out_specs=[out_spec] * 3 + [four_spec] * 2,
        out_shape=[half] * 3 + [four] * 2,
        compiler_params=_cparams(vmem, 1),
        name="proj",
    )(x, w_in, cs)


PAIR_Q = 2 * GRID_W
PAIR_KROWS = WIN_ROWS + 2
PAIR_K = PAIR_KROWS * GRID_W
ATT_PAIRS_PER_STEP = 8
ATT_PAIRS_PER_ITER = 4
N_TZ = 2 * WIN_ROWS
LOG2E = math.log2(math.e)


def _toeplitz_tables(rpb_l):
    cols = np.arange(GRID_W)
    c0 = np.clip(cols - WIN_COLS // 2, 0, GRID_W - WIN_COLS)
    in_win = (cols[None, :] >= c0[:, None]) & (cols[None, :] < c0[:, None] + WIN_COLS)
    dc_idx = np.clip(cols[None, :] - cols[:, None], -(WIN_COLS - 1), WIN_COLS - 1) + (WIN_COLS - 1)
    onehot = (dc_idx[None] == np.arange(2 * WIN_COLS - 1)[:, None, None]).astype(np.float32)
    tz = jnp.einsum("hdj,jqk->hdqk", rpb_l.astype(F32) * LOG2E, jnp.asarray(onehot),
                    precision=lax.Precision.HIGHEST)
    tz = jnp.where(in_win, tz, NEG_INF)
    tzp = jnp.pad(tz, ((0, 0), (1, 1), (0, 0), (0, 0)), constant_values=NEG_INF)
    left, right = tzp[:, :-1], tzp[:, 1:]
    masked = jnp.full_like(left, NEG_INF)
    return jnp.concatenate([jnp.concatenate([left, right], axis=-1),
                            jnp.concatenate([left, masked], axis=-1),
                            jnp.concatenate([masked, right], axis=-1)], axis=1)


def _nattn_kernel(q_ref, k_ref, v_ref, tz_ref, o_ref, *, n_rows):
    step = pl.program_id(1)
    lane = lax.broadcasted_iota(jnp.int32, (PAIR_Q, V7X_LANES), 1)
    first_head = lane < HEAD_DIM

    def pair_geometry(pl_idx):
        p = step * ATT_PAIRS_PER_STEP + pl_idx
        w = jnp.clip(2 * p - WIN_ROWS // 2, 0, n_rows - PAIR_KROWS)

        def tz_index(a, t):
            r = 2 * p + a
            r0 = jnp.clip(r - WIN_ROWS // 2, 0, n_rows - WIN_ROWS)
            kr = w + 2 * t
            ok0 = (kr >= r0) & (kr < r0 + WIN_ROWS)
            ok1 = (kr + 1 >= r0) & (kr + 1 < r0 + WIN_ROWS)
            i = jnp.clip(kr - r + WIN_ROWS, 0, N_TZ - 1)
            return jnp.where(ok0 & ok1, i,
                             jnp.where(ok0, N_TZ + i, jnp.where(ok1, 2 * N_TZ + i, N_TZ)))

        return dict(
            qrows=pl.ds(pl.multiple_of(pl_idx * PAIR_Q, PAIR_Q), PAIR_Q),
            krows=pl.ds(pl.multiple_of(w * GRID_W, V7X_LANES), PAIR_K),
            tz_idx=[[tz_index(a, t) for t in range(PAIR_KROWS // 2)] for a in (0, 1)])

    def scores(geo, hp):
        lanes = slice(hp * V7X_LANES, (hp + 1) * V7X_LANES)
        qp = q_ref[geo["qrows"], lanes]
        kp = k_ref[geo["krows"], lanes]
        zero = jnp.zeros_like(qp)
        qq = jnp.concatenate([jnp.where(first_head, qp, zero),
                              jnp.where(first_head, zero, qp)], axis=0)
        return lax.dot_general(qq, kp, (((1,), (1,)), ((), ())),
                               preferred_element_type=F32)

    def finish(geo, hp, s):
        lanes = slice(hp * V7X_LANES, (hp + 1) * V7X_LANES)
        probs = []
        denom = []
        for e in (0, 1):
            h = 2 * hp + e
            bias = jnp.concatenate(
                [jnp.concatenate([tz_ref[h, geo["tz_idx"][a][t]] for t in range(PAIR_KROWS // 2)],
                                 axis=1) for a in (0, 1)], axis=0)
            se = s[e * PAIR_Q:(e + 1) * PAIR_Q] + bias
            m = jnp.max(se, axis=-1, keepdims=True)
            pe = jnp.exp2(se - m)
            denom.append(jnp.sum(pe, axis=-1, keepdims=True))
            probs.append(pe.astype(BF16))
        o = _dot(jnp.concatenate(probs, axis=0), v_ref[geo["krows"], lanes])
        o0 = o[:PAIR_Q] / denom[0]
        o1 = o[PAIR_Q:] / denom[1]
        o_ref[geo["qrows"], lanes] = jnp.where(first_head, o0, o1)

    def body(it, carry):
        geos = [pair_geometry(it * ATT_PAIRS_PER_ITER + j) for j in range(ATT_PAIRS_PER_ITER)]
        units = [(geos[j], hp) for j in range(ATT_PAIRS_PER_ITER)
                 for hp in range(N_ATTN_HEADS // 2)]
        s_cur = scores(*units[0])
        for u in range(1, len(units)):
            s_next = scores(*units[u])
            finish(*units[u - 1], s_cur)
            s_cur = s_next
        finish(*units[-1], s_cur)
        return carry

    lax.fori_loop(0, ATT_PAIRS_PER_STEP // ATT_PAIRS_PER_ITER, body, 0)


def _nattn(q, k, v, tz2, batch, seq):
    n_rows = seq // GRID_W
    tq = ATT_PAIRS_PER_STEP * PAIR_Q
    steps = seq // tq
    k3 = k.reshape(batch, seq, D_ATTN)
    v3 = v.reshape(batch, seq, D_ATTN)
    vmem = (2 * 2 * seq * D_ATTN * 2
            + tz2.size * 4
            + 2 * tq * D_ATTN * (2 + 4)
            + (8 << 20))
    kv_spec = pl.BlockSpec((None, seq, D_ATTN), lambda b, i: (b, 0, 0))
    return pl.pallas_call(
        functools.partial(_nattn_kernel, n_rows=n_rows),
        grid=(batch, steps),
        in_specs=[
            pl.BlockSpec((tq, D_ATTN), lambda b, i: (b * steps + i, 0)),
            kv_spec,
            kv_spec,
            _const_spec(tz2.shape),
        ],
        out_specs=pl.BlockSpec((tq, D_ATTN), lambda b, i: (b * steps + i, 0)),
        out_shape=jax.ShapeDtypeStruct((batch * seq, D_ATTN), F32),
        compiler_params=_cparams(vmem, 2),
        name="nattn",
    )(q, k3, v3, tz2)


SEQ_N1 = 128
SEQ_N2 = 64
FN1_S2_PER_STEP = 8
FN2_K1_PER_STEP = 8


FN1_ROWS = SEQ_N1 * FN1_S2_PER_STEP


def _fnet1_kernel(a_ref, b_ref, w1_ref, tc_ref, ts_ref, pr_ref, pi_ref):
    flat = (N_FOURIER_GROUPS * FN1_ROWS, FOURIER_GROUP)
    a2, b2 = a_ref.reshape(*flat), b_ref.reshape(*flat)
    pr2, pi2 = pr_ref.reshape(*flat), pi_ref.reshape(*flat)
    tcos = tc_ref[...]
    tsin = ts_ref[...]

    def rows(grp, j):
        return pl.ds(grp * FN1_ROWS + j, SEQ_N1, stride=FN1_S2_PER_STEP)

    def gather(ref2, j):
        return jnp.concatenate([ref2[rows(grp, j), :] for grp in range(N_FOURIER_GROUPS)], axis=1)

    for j in range(FN1_S2_PER_STEP):
        x = jnp.concatenate([gather(a2, j).astype(BF16), gather(b2, j).astype(BF16)],
                            axis=0)
        t = _dot(w1_ref[...], x)
        tr = t[:SEQ_N1]
        ti = t[SEQ_N1:]
        c = tcos[:, j:j + 1]
        s = tsin[:, j:j + 1]
        pr = tr * c + ti * s
        pi = ti * c - tr * s
        for grp in range(N_FOURIER_GROUPS):
            cols = slice(grp * FOURIER_GROUP, (grp + 1) * FOURIER_GROUP)
            pr2[rows(grp, j), :] = pr[:, cols]
            pi2[rows(grp, j), :] = pi[:, cols]


def _fnet1(a, b, w1, tcos, tsin, batch, seq):
    shape5 = (N_FOURIER_GROUPS, batch, SEQ_N1, SEQ_N2, FOURIER_GROUP)
    a5 = a.reshape(shape5)
    b5 = b.reshape(shape5)
    nblk = SEQ_N2 // FN1_S2_PER_STEP
    io_spec = pl.BlockSpec((N_FOURIER_GROUPS, None, SEQ_N1, FN1_S2_PER_STEP, FOURIER_GROUP),
                           lambda bb, j: (0, bb, 0, j, 0))
    tw_spec = pl.BlockSpec((None, SEQ_N1, FN1_S2_PER_STEP), lambda bb, j: (j, 0, 0))
    shape = jax.ShapeDtypeStruct(shape5, F32)
    blk_bytes = N_FOURIER_GROUPS * FN1_ROWS * FOURIER_GROUP * 4
    vmem = 8 * blk_bytes + (8 << 20)
    return pl.pallas_call(
        _fnet1_kernel,
        grid=(batch, nblk),
        in_specs=[io_spec, io_spec, _const_spec((2 * SEQ_N1, 2 * SEQ_N1)), tw_spec, tw_spec],
        out_specs=[io_spec, io_spec],
        out_shape=[shape, shape],
        compiler_params=_cparams(vmem, 2),
        name="fnet1",
    )(a5, b5, w1, tcos, tsin)


def _fnet2_kernel(pr_ref, pi_ref, w2_ref, y_ref):
    def slab(ref, j):
        rows = slice(j * SEQ_N2, (j + 1) * SEQ_N2)
        return jnp.concatenate([ref[grp, rows, :] for grp in range(N_FOURIER_GROUPS)],
                               axis=1).astype(BF16)

    for j in range(FN2_K1_PER_STEP):
        x = jnp.concatenate([slab(pr_ref, j), slab(pi_ref, j)], axis=0)
        y_ref[:, j, :] = _dot(w2_ref[...], x)


def _fnet2(pr, pi, w2, batch, seq):
    shape4 = (N_FOURIER_GROUPS, batch, seq, FOURIER_GROUP)
    pr3 = pr.reshape(shape4)
    pi3 = pi.reshape(shape4)
    rows = FN2_K1_PER_STEP * SEQ_N2
    in_spec = pl.BlockSpec((N_FOURIER_GROUPS, None, rows, FOURIER_GROUP),
                           lambda bb, j: (0, bb, j, 0))
    vmem = 4 * rows * D_FOURIER * 4 + 2 * SEQ_N2 * FN2_K1_PER_STEP * D_FOURIER * 4 + (8 << 20)
    y = pl.pallas_call(
        _fnet2_kernel,
        grid=(batch, SEQ_N1 // FN2_K1_PER_STEP),
        in_specs=[in_spec, in_spec, _const_spec((SEQ_N2, 2 * SEQ_N2))],
        out_specs=pl.BlockSpec((None, SEQ_N2, FN2_K1_PER_STEP, D_FOURIER),
                               lambda bb, j: (bb, 0, j, 0)),
        out_shape=jax.ShapeDtypeStruct((batch, SEQ_N2, SEQ_N1, D_FOURIER), F32),
        compiler_params=_cparams(vmem, 2),
        name="fnet2",
    )(pr3, pi3, w2)
    return y.reshape(batch * seq, D_FOURIER)


def _dft_constants():
    def cs(n):
        idx = np.arange(n)
        ang = 2.0 * np.pi * ((idx[:, None] * idx[None, :]) % n) / n
        return np.cos(ang), np.sin(ang)

    cc, sc = cs(FOURIER_GROUP)
    chan = np.concatenate([cc, sc], axis=1)
    c1, s1 = cs(SEQ_N1)
    w1 = np.block([[c1, -s1], [-s1, -c1]])
    c2, s2 = cs(SEQ_N2)
    norm = 1.0 / math.sqrt(SEQ_N1 * SEQ_N2 * FOURIER_GROUP)
    w2 = np.concatenate([c2, s2], axis=1) * norm
    k1 = np.arange(SEQ_N1)
    s2i = np.arange(SEQ_N2)
    ang = 2.0 * np.pi * (k1[:, None] * s2i[None, :]) / (SEQ_N1 * SEQ_N2)
    nblk = SEQ_N2 // FN1_S2_PER_STEP

    def blocked(t):
        return t.reshape(SEQ_N1, nblk, FN1_S2_PER_STEP).transpose(1, 0, 2)

    return (jnp.asarray(chan, F32).astype(BF16), jnp.asarray(w1, F32).astype(BF16),
            jnp.asarray(w2, F32).astype(BF16),
            jnp.asarray(blocked(np.cos(ang)), F32), jnp.asarray(blocked(np.sin(ang)), F32))


MERGE_TM = 512


def _merge_ln_kernel(attn_ref, four_ref, x_ref, ga_ref, gf_ref, wo_ref, g_ref, b_ref, o_ref):
    ma = _rms_norm(attn_ref[...], ga_ref[...]).astype(BF16)
    mf = _rms_norm(four_ref[...], gf_ref[...]).astype(BF16)
    z = _dot(ma, wo_ref[0:D_ATTN, :]) + _dot(mf, wo_ref[D_ATTN:, :])
    o_ref[...] = _layer_norm(ALPHA * x_ref[...] + z, g_ref[...], b_ref[...])


def _merge_ln(attn, four, x, ga, gf, wo, g, b):
    n = x.shape[0]
    tm = MERGE_TM
    vmem = (D_MODEL * D_MODEL * 2 + 4 * tm * D_ATTN * 4 + 4 * tm * D_MODEL * 4
            + 6 * tm * D_MODEL * 4)
    return pl.pallas_call(
        _merge_ln_kernel,
        grid=(n // tm,),
        in_specs=[
            pl.BlockSpec((tm, D_ATTN), lambda i: (i, 0)),
            pl.BlockSpec((tm, D_FOURIER), lambda i: (i, 0)),
            pl.BlockSpec((tm, D_MODEL), lambda i: (i, 0)),
            _const_spec((1, D_ATTN)),
            _const_spec((1, D_FOURIER)),
            _const_spec((D_MODEL, D_MODEL)),
            _const_spec((1, D_MODEL)),
            _const_spec((1, D_MODEL)),
        ],
        out_specs=pl.BlockSpec((tm, D_MODEL), lambda i: (i, 0)),
        out_shape=jax.ShapeDtypeStruct((n, D_MODEL), F32),
        compiler_params=_cparams(vmem, 1),
        name="merge_ln",
    )(attn, four, x, ga, gf, wo, g, b)


def kernel(x, ffn1_w_gate, ffn1_w_up, ffn1_w_down, ln1_g, ln1_b, w_in, rpb, g_attn, g_fourier,
           w_out, ln2_g, ln2_b, ffn2_w_gate, ffn2_w_up, ffn2_w_down, ln3_g, ln3_b):
    batch, seq, d = x.shape
    assert d == D_MODEL and seq == SEQ_N1 * SEQ_N2 and seq % (ATT_PAIRS_PER_STEP * PAIR_Q) == 0
    assert rpb.shape[0] == DEPTH
    n_rows = seq // GRID_W
    chan, w1, w2, tcos, tsin = _dft_constants()

    def row(p, l):
        return p[l].reshape(1, -1).astype(F32)

    h = x.reshape(batch * seq, d).astype(F32)
    for l in range(DEPTH):
        h = _ffn_ln(h, ffn1_w_gate[l].astype(BF16), ffn1_w_up[l].astype(BF16),
                    ffn1_w_down[l].astype(BF16), row(ln1_g, l), row(ln1_b, l))
        q, k, v, a, b = _proj(h, w_in[l].astype(BF16), chan)
        attn = _nattn(q, k, v, _toeplitz_tables(rpb[l]), batch, seq)
        pr, pi = _fnet1(a, b, w1, tcos, tsin, batch, seq)
        four = _fnet2(pr, pi, w2, batch, seq)
        h = _merge_ln(attn, four, h, row(g_attn, l), row(g_fourier, l), w_out[l].astype(BF16),
                      row(ln2_g, l), row(ln2_b, l))
        h = _ffn_ln(h, ffn2_w_gate[l].astype(BF16), ffn2_w_up[l].astype(BF16),
                    ffn2_w_down[l].astype(BF16), row(ln3_g, l), row(ln3_b, l))
    return h.reshape(batch, seq, d).astype(x.dtype)
```

```python
import functools
import math

import jax
import jax.numpy as jnp
import numpy as np
from jax import lax
from jax.experimental import pallas as pl
from jax.experimental.pallas import tpu as pltpu

D_MODEL = 1024
DEPTH = 2
GRID_W = 64
WIN_ROWS = 8
WIN_COLS = 16
N_ATTN_HEADS = 8
HEAD_DIM = 64
D_ATTN = N_ATTN_HEADS * HEAD_DIM
N_FOURIER_GROUPS = 4
FOURIER_GROUP = 128
D_FOURIER = N_FOURIER_GROUPS * FOURIER_GROUP
D_IN = 3 * D_ATTN + D_FOURIER
D_FF = 2816
ALPHA = (2.0 * DEPTH) ** 0.25
LN_EPS = 1e-5
RMS_EPS = 1e-6
NEG_INF = -1e30

V7X_LANES = 128
V7X_MXU_DIM = 256
V7X_VMEM_BYTES = 64 * 1024 * 1024

F32 = jnp.float32
BF16 = jnp.bfloat16


def _cparams(vmem_bytes, n_axes):
    return pltpu.CompilerParams(
        dimension_semantics=("arbitrary",) * n_axes,
        vmem_limit_bytes=min(int(vmem_bytes), V7X_VMEM_BYTES - (2 << 20)),
    )


def _const_spec(shape):
    nd = len(shape)
    return pl.BlockSpec(shape, lambda *_: (0,) * nd, pipeline_mode=pl.Buffered(1))


def _layer_norm(z, g, b):
    mu = jnp.mean(z, axis=-1, keepdims=True)
    zc = z - mu
    var = jnp.mean(zc * zc, axis=-1, keepdims=True)
    return zc * lax.rsqrt(var + LN_EPS) * g + b


def _rms_norm(z, g):
    return z * lax.rsqrt(jnp.mean(z * z, axis=-1, keepdims=True) + RMS_EPS) * g


def _dot(a, b):
    return jnp.dot(a, b, preferred_element_type=F32)


TOK_TM = 512
FFN_FC = V7X_MXU_DIM
Q_SCALE = HEAD_DIM ** -0.5 * math.log2(math.e)


def _swiglu_ln(x, wg_ref, wu_ref, wd_ref, g_ref, b_ref, h_ref):
    xb = x.astype(BF16)
    for c in range(D_FF // FFN_FC):
        cols = slice(c * FFN_FC, (c + 1) * FFN_FC)
        gate = _dot(xb, wg_ref[:, cols])
        up = _dot(xb, wu_ref[:, cols])
        h_ref[:, cols] = (gate * jax.nn.sigmoid(gate) * up).astype(BF16)
    y = _dot(h_ref[...], wd_ref[...])
    return _layer_norm(ALPHA * x + 0.5 * y, g_ref[...], b_ref[...])


def _ffn_proj_kernel(x_ref, wg_ref, wu_ref, wd_ref, g_ref, b_ref, win_ref, cs_ref,
                     x1_ref, q_ref, k_ref, v_ref, a_ref, bb_ref, h_ref):
    x1 = _swiglu_ln(x_ref[...], wg_ref, wu_ref, wd_ref, g_ref, b_ref, h_ref)
    x1_ref[...] = x1
    xb = x1.astype(BF16)
    q_ref[...] = (_dot(xb, win_ref[:, 0:D_ATTN]) * Q_SCALE).astype(BF16)
    k_ref[...] = _dot(xb, win_ref[:, D_ATTN:2 * D_ATTN]).astype(BF16)
    v_ref[...] = _dot(xb, win_ref[:, 2 * D_ATTN:3 * D_ATTN]).astype(BF16)
    u = _dot(xb, win_ref[:, 3 * D_ATTN:]).astype(BF16)
    for grp in range(N_FOURIER_GROUPS):
        cols = slice(grp * FOURIER_GROUP, (grp + 1) * FOURIER_GROUP)
        ab = _dot(u[:, cols], cs_ref[...])
        a_ref[grp] = ab[:, :FOURIER_GROUP]
        bb_ref[grp] = ab[:, FOURIER_GROUP:]


def _mix_ffn_kernel(attn_ref, four_ref, x_ref, ga_ref, gf_ref, wo_ref, g2_ref, b2_ref,
                    wg_ref, wu_ref, wd_ref, g3_ref, b3_ref, o_ref, h_ref):
    ma = _rms_norm(attn_ref[...], ga_ref[...]).astype(BF16)
    mf = _rms_norm(four_ref[...], gf_ref[...]).astype(BF16)
    z = _dot(ma, wo_ref[0:D_ATTN, :]) + _dot(mf, wo_ref[D_ATTN:, :])
    x2 = _layer_norm(ALPHA * x_ref[...] + z, g2_ref[...], b2_ref[...])
    o_ref[...] = _swiglu_ln(x2, wg_ref, wu_ref, wd_ref, g3_ref, b3_ref, h_ref)


def _tok_spec(width):
    return pl.BlockSpec((TOK_TM, width), lambda i: (i, 0))


_FFN_WEIGHT_SHAPES = [(D_MODEL, D_FF), (D_MODEL, D_FF), (D_FF, D_MODEL), (1, D_MODEL), (1, D_MODEL)]
_FFN_VMEM = 3 * D_MODEL * D_FF * 2 + TOK_TM * D_FF * 2 + 8 * TOK_TM * D_MODEL * 4


def _ffn_proj(x, wg, wu, wd, g, b, w_in, cs):
    n = x.shape[0]
    half = jax.ShapeDtypeStruct((n, D_ATTN), BF16)
    four = jax.ShapeDtypeStruct((N_FOURIER_GROUPS, n, FOURIER_GROUP), F32)
    four_spec = pl.BlockSpec((N_FOURIER_GROUPS, TOK_TM, FOURIER_GROUP), lambda i: (0, i, 0))
    vmem = (_FFN_VMEM + D_MODEL * D_IN * 2
            + 2 * TOK_TM * (2 * D_MODEL * 4 + 3 * D_ATTN * 2 + 2 * D_FOURIER * 4)
            + 2 * TOK_TM * D_IN * 4)
    return pl.pallas_call(
        _ffn_proj_kernel,
        grid=(n // TOK_TM,),
        in_specs=([_tok_spec(D_MODEL)] + [_const_spec(s) for s in _FFN_WEIGHT_SHAPES]
                  + [_const_spec((D_MODEL, D_IN)),
                     _const_spec((FOURIER_GROUP, 2 * FOURIER_GROUP))]),
        out_specs=[_tok_spec(D_MODEL)] + [_tok_spec(D_ATTN)] * 3 + [four_spec] * 2,
        out_shape=[jax.ShapeDtypeStruct((n, D_MODEL), F32)] + [half] * 3 + [four] * 2,
        scratch_shapes=[pltpu.VMEM((TOK_TM, D_FF), BF16)],
        compiler_params=_cparams(vmem, 1),
        name="ffn_proj",
    )(x, wg, wu, wd, g, b, w_in, cs)


def _mix_ffn(attn, four, x, ga, gf, wo, g2, b2, wg, wu, wd, g3, b3):
    n = x.shape[0]
    vmem = (_FFN_VMEM + D_MODEL * D_MODEL * 2
            + 2 * TOK_TM * (2 * D_MODEL * 4 + D_ATTN * 4 + D_FOURIER * 4))
    return pl.pallas_call(
        _mix_ffn_kernel,
        grid=(n // TOK_TM,),
        in_specs=([_tok_spec(D_ATTN), _tok_spec(D_FOURIER), _tok_spec(D_MODEL),
                   _const_spec((1, D_ATTN)), _const_spec((1, D_FOURIER)),
                   _const_spec((D_MODEL, D_MODEL)), _const_spec((1, D_MODEL)),
                   _const_spec((1, D_MODEL))]
                  + [_const_spec(s) for s in _FFN_WEIGHT_SHAPES]),
        out_specs=_tok_spec(D_MODEL),
        out_shape=jax.ShapeDtypeStruct((n, D_MODEL), F32),
        scratch_shapes=[pltpu.VMEM((TOK_TM, D_FF), BF16)],
        compiler_params=_cparams(vmem, 1),
        name="mix_ffn",
    )(attn, four, x, ga, gf, wo, g2, b2, wg, wu, wd, g3, b3)


PAIR_Q = 2 * GRID_W
PAIR_KROWS = WIN_ROWS + 2
PAIR_K = PAIR_KROWS * GRID_W
ATT_PAIRS_PER_STEP = 8
ATT_PAIRS_PER_ITER = 4
N_TZ = 2 * WIN_ROWS
LOG2E = math.log2(math.e)


def _toeplitz_tables(rpb_l):
    cols = np.arange(GRID_W)
    c0 = np.clip(cols - WIN_COLS // 2, 0, GRID_W - WIN_COLS)
    in_win = (cols[None, :] >= c0[:, None]) & (cols[None, :] < c0[:, None] + WIN_COLS)
    dc_idx = np.clip(cols[None, :] - cols[:, None], -(WIN_COLS - 1), WIN_COLS - 1) + (WIN_COLS - 1)
    onehot = (dc_idx[None] == np.arange(2 * WIN_COLS - 1)[:, None, None]).astype(np.float32)
    tz = jnp.einsum("hdj,jqk->hdqk", rpb_l.astype(F32) * LOG2E, jnp.asarray(onehot),
                    precision=lax.Precision.HIGHEST)
    tz = jnp.where(in_win, tz, NEG_INF)
    tzp = jnp.pad(tz, ((0, 0), (1, 1), (0, 0), (0, 0)), constant_values=NEG_INF)
    left, right = tzp[:, :-1], tzp[:, 1:]
    masked = jnp.full_like(left, NEG_INF)
    return jnp.concatenate([jnp.concatenate([left, right], axis=-1),
                            jnp.concatenate([left, masked], axis=-1),
                            jnp.concatenate([masked, right], axis=-1)], axis=1)


def _nattn_kernel(q_ref, k_ref, v_ref, tz_ref, o_ref, *, n_rows):
    step = pl.program_id(1)
    lane = lax.broadcasted_iota(jnp.int32, (PAIR_Q, V7X_LANES), 1)
    first_head = lane < HEAD_DIM

    def pair_geometry(pl_idx):
        p = step * ATT_PAIRS_PER_STEP + pl_idx
        w = jnp.clip(2 * p - WIN_ROWS // 2, 0, n_rows - PAIR_KROWS)

        def tz_index(a, t):
            r = 2 * p + a
            r0 = jnp.clip(r - WIN_ROWS // 2, 0, n_rows - WIN_ROWS)
            kr = w + 2 * t
            ok0 = (kr >= r0) & (kr < r0 + WIN_ROWS)
            ok1 = (kr + 1 >= r0) & (kr + 1 < r0 + WIN_ROWS)
            i = jnp.clip(kr - r + WIN_ROWS, 0, N_TZ - 1)
            return jnp.where(ok0 & ok1, i,
                             jnp.where(ok0, N_TZ + i, jnp.where(ok1, 2 * N_TZ + i, N_TZ)))

        return dict(
            qrows=pl.ds(pl.multiple_of(pl_idx * PAIR_Q, PAIR_Q), PAIR_Q),
            krows=pl.ds(pl.multiple_of(w * GRID_W, V7X_LANES), PAIR_K),
            tz_idx=[[tz_index(a, t) for t in range(PAIR_KROWS // 2)] for a in (0, 1)])

    def scores(geo, hp):
        lanes = slice(hp * V7X_LANES, (hp + 1) * V7X_LANES)
        qp = q_ref[geo["qrows"], lanes]
        kp = k_ref[geo["krows"], lanes]
        zero = jnp.zeros_like(qp)
        qq = jnp.concatenate([jnp.where(first_head, qp, zero),
                              jnp.where(first_head, zero, qp)], axis=0)
        return lax.dot_general(qq, kp, (((1,), (1,)), ((), ())),
                               preferred_element_type=F32)

    def finish(geo, hp, s):
        lanes = slice(hp * V7X_LANES, (hp + 1) * V7X_LANES)
        probs = []
        denom = []
        for e in (0, 1):
            h = 2 * hp + e
            bias = jnp.concatenate(
                [jnp.concatenate([tz_ref[h, geo["tz_idx"][a][t]] for t in range(PAIR_KROWS // 2)],
                                 axis=1) for a in (0, 1)], axis=0)
            se = s[e * PAIR_Q:(e + 1) * PAIR_Q] + bias
            m = jnp.max(se, axis=-1, keepdims=True)
            pe = jnp.exp2(se - m)
            denom.append(jnp.sum(pe, axis=-1, keepdims=True))
            probs.append(pe.astype(BF16))
        o = _dot(jnp.concatenate(probs, axis=0), v_ref[geo["krows"], lanes])
        o0 = o[:PAIR_Q] / denom[0]
        o1 = o[PAIR_Q:] / denom[1]
        o_ref[geo["qrows"], lanes] = jnp.where(first_head, o0, o1)

    def body(it, carry):
        geos = [pair_geometry(it * ATT_PAIRS_PER_ITER + j) for j in range(ATT_PAIRS_PER_ITER)]
        units = [(geos[j], hp) for j in range(ATT_PAIRS_PER_ITER)
                 for hp in range(N_ATTN_HEADS // 2)]
        s_cur = scores(*units[0])
        for u in range(1, len(units)):
            s_next = scores(*units[u])
            finish(*units[u - 1], s_cur)
            s_cur = s_next
        finish(*units[-1], s_cur)
        return carry

    lax.fori_loop(0, ATT_PAIRS_PER_STEP // ATT_PAIRS_PER_ITER, body, 0)


def _nattn(q, k, v, tz2, batch, seq):
    n_rows = seq // GRID_W
    tq = ATT_PAIRS_PER_STEP * PAIR_Q
    steps = seq // tq
    k3 = k.reshape(batch, seq, D_ATTN)
    v3 = v.reshape(batch, seq, D_ATTN)
    vmem = (2 * 2 * seq * D_ATTN * 2
            + tz2.size * 4
            + 2 * tq * D_ATTN * (2 + 4)
            + (8 << 20))
    kv_spec = pl.BlockSpec((None, seq, D_ATTN), lambda b, i: (b, 0, 0))
    return pl.pallas_call(
        functools.partial(_nattn_kernel, n_rows=n_rows),
        grid=(batch, steps),
        in_specs=[
            pl.BlockSpec((tq, D_ATTN), lambda b, i: (b * steps + i, 0)),
            kv_spec,
            kv_spec,
            _const_spec(tz2.shape),
        ],
        out_specs=pl.BlockSpec((tq, D_ATTN), lambda b, i: (b * steps + i, 0)),
        out_shape=jax.ShapeDtypeStruct((batch * seq, D_ATTN), F32),
        compiler_params=_cparams(vmem, 2),
        name="nattn",
    )(q, k3, v3, tz2)


SEQ_N1 = 128
SEQ_N2 = 64
FN1_S2_PER_STEP = 8
FN2_K1_PER_STEP = 8
FN1_ROWS = SEQ_N1 * FN1_S2_PER_STEP


def _fnet1_kernel(a_ref, b_ref, w1_ref, tc_ref, ts_ref, pr_ref, pi_ref):
    flat = (N_FOURIER_GROUPS * FN1_ROWS, FOURIER_GROUP)
    a2, b2 = a_ref.reshape(*flat), b_ref.reshape(*flat)
    pr2, pi2 = pr_ref.reshape(*flat), pi_ref.reshape(*flat)
    tcos = tc_ref[...]
    tsin = ts_ref[...]

    def rows(grp, j):
        return pl.ds(grp * FN1_ROWS + j, SEQ_N1, stride=FN1_S2_PER_STEP)

    def gather(ref2, j):
        return jnp.concatenate([ref2[rows(grp, j), :] for grp in range(N_FOURIER_GROUPS)], axis=1)

    for j in range(FN1_S2_PER_STEP):
        x = jnp.concatenate([gather(a2, j).astype(BF16), gather(b2, j).astype(BF16)],
                            axis=0)
        t = _dot(w1_ref[...], x)
        tr = t[:SEQ_N1]
        ti = t[SEQ_N1:]
        c = tcos[:, j:j + 1]
        s = tsin[:, j:j + 1]
        pr = tr * c + ti * s
        pi = ti * c - tr * s
        for grp in range(N_FOURIER_GROUPS):
            cols = slice(grp * FOURIER_GROUP, (grp + 1) * FOURIER_GROUP)
            pr2[rows(grp, j), :] = pr[:, cols]
            pi2[rows(grp, j), :] = pi[:, cols]


def _fnet1(a, b, w1, tcos, tsin, batch, seq):
    shape5 = (N_FOURIER_GROUPS, batch, SEQ_N1, SEQ_N2, FOURIER_GROUP)
    a5 = a.reshape(shape5)
    b5 = b.reshape(shape5)
    nblk = SEQ_N2 // FN1_S2_PER_STEP
    io_spec = pl.BlockSpec((N_FOURIER_GROUPS, None, SEQ_N1, FN1_S2_PER_STEP, FOURIER_GROUP),
                           lambda bb, j: (0, bb, 0, j, 0))
    tw_spec = pl.BlockSpec((None, SEQ_N1, FN1_S2_PER_STEP), lambda bb, j: (j, 0, 0))
    shape = jax.ShapeDtypeStruct(shape5, F32)
    blk_bytes = N_FOURIER_GROUPS * FN1_ROWS * FOURIER_GROUP * 4
    vmem = 8 * blk_bytes + (8 << 20)
    return pl.pallas_call(
        _fnet1_kernel,
        grid=(batch, nblk),
        in_specs=[io_spec, io_spec, _const_spec((2 * SEQ_N1, 2 * SEQ_N1)), tw_spec, tw_spec],
        out_specs=[io_spec, io_spec],
        out_shape=[shape, shape],
        compiler_params=_cparams(vmem, 2),
        name="fnet1",
    )(a5, b5, w1, tcos, tsin)


def _fnet2_kernel(pr_ref, pi_ref, w2_ref, y_ref):
    def slab(ref, j):
        rows = slice(j * SEQ_N2, (j + 1) * SEQ_N2)
        return jnp.concatenate([ref[grp, rows, :] for grp in range(N_FOURIER_GROUPS)],
                               axis=1).astype(BF16)

    for j in range(FN2_K1_PER_STEP):
        x = jnp.concatenate([slab(pr_ref, j), slab(pi_ref, j)], axis=0)
        y_ref[:, j, :] = _dot(w2_ref[...], x)


def _fnet2(pr, pi, w2, batch, seq):
    shape4 = (N_FOURIER_GROUPS, batch, seq, FOURIER_GROUP)
    pr3 = pr.reshape(shape4)
    pi3 = pi.reshape(shape4)
    rows = FN2_K1_PER_STEP * SEQ_N2
    in_spec = pl.BlockSpec((N_FOURIER_GROUPS, None, rows, FOURIER_GROUP),
                           lambda bb, j: (0, bb, j, 0))
    vmem = 4 * rows * D_FOURIER * 4 + 2 * SEQ_N2 * FN2_K1_PER_STEP * D_FOURIER * 4 + (8 << 20)
    y = pl.pallas_call(
        _fnet2_kernel,
        grid=(batch, SEQ_N1 // FN2_K1_PER_STEP),
        in_specs=[in_spec, in_spec, _const_spec((SEQ_N2, 2 * SEQ_N2))],
        out_specs=pl.BlockSpec((None, SEQ_N2, FN2_K1_PER_STEP, D_FOURIER),
                               lambda bb, j: (bb, 0, j, 0)),
        out_shape=jax.ShapeDtypeStruct((batch, SEQ_N2, SEQ_N1, D_FOURIER), F32),
        compiler_params=_cparams(vmem, 2),
        name="fnet2",
    )(pr3, pi3, w2)
    return y.reshape(batch * seq, D_FOURIER)


def _dft_constants():
    def cs(n):
        idx = np.arange(n)
        ang = 2.0 * np.pi * ((idx[:, None] * idx[None, :]) % n) / n
        return np.cos(ang), np.sin(ang)

    cc, sc = cs(FOURIER_GROUP)
    chan = np.concatenate([cc, sc], axis=1)
    c1, s1 = cs(SEQ_N1)
    w1 = np.block([[c1, -s1], [-s1, -c1]])
    c2, s2 = cs(SEQ_N2)
    norm = 1.0 / math.sqrt(SEQ_N1 * SEQ_N2 * FOURIER_GROUP)
    w2 = np.concatenate([c2, s2], axis=1) * norm
    k1 = np.arange(SEQ_N1)
    s2i = np.arange(SEQ_N2)
    ang = 2.0 * np.pi * (k1[:, None] * s2i[None, :]) / (SEQ_N1 * SEQ_N2)
    nblk = SEQ_N2 // FN1_S2_PER_STEP

    def blocked(t):
        return t.reshape(SEQ_N1, nblk, FN1_S2_PER_STEP).transpose(1, 0, 2)

    return (jnp.asarray(chan, F32).astype(BF16), jnp.asarray(w1, F32).astype(BF16),
            jnp.asarray(w2, F32).astype(BF16),
            jnp.asarray(blocked(np.cos(ang)), F32), jnp.asarray(blocked(np.sin(ang)), F32))


def kernel(x, ffn1_w_gate, ffn1_w_up, ffn1_w_down, ln1_g, ln1_b, w_in, rpb, g_attn, g_fourier,
           w_out, ln2_g, ln2_b, ffn2_w_gate, ffn2_w_up, ffn2_w_down, ln3_g, ln3_b):
    batch, seq, d = x.shape
    assert d == D_MODEL and seq == SEQ_N1 * SEQ_N2 and seq % (ATT_PAIRS_PER_STEP * PAIR_Q) == 0
    assert rpb.shape[0] == DEPTH
    chan, w1, w2, tcos, tsin = _dft_constants()

    def row(p, l):
        return p[l].reshape(1, -1).astype(F32)

    h = x.reshape(batch * seq, d).astype(F32)
    for l in range(DEPTH):
        h, q, k, v, a, b = _ffn_proj(
            h, ffn1_w_gate[l].astype(BF16), ffn1_w_up[l].astype(BF16),
            ffn1_w_down[l].astype(BF16), row(ln1_g, l), row(ln1_b, l),
            w_in[l].astype(BF16), chan)
        attn = _nattn(q, k, v, _toeplitz_tables(rpb[l]), batch, seq)
        pr, pi = _fnet1(a, b, w1, tcos, tsin, batch, seq)
        four = _fnet2(pr, pi, w2, batch, seq)
        h = _mix_ffn(attn, four, h, row(g_attn, l), row(g_fourier, l), w_out[l].astype(BF16),
                     row(ln2_g, l), row(ln2_b, l),
                     ffn2_w_gate[l].astype(BF16), ffn2_w_up[l].astype(BF16),
                     ffn2_w_down[l].astype(BF16), row(ln3_g, l), row(ln3_b, l))
    return h.reshape(batch, seq, d).astype(x.dtype)
```

```python
import functools
import math

import jax
import jax.numpy as jnp
import numpy as np
from jax import lax
from jax.experimental import pallas as pl
from jax.experimental.pallas import tpu as pltpu

D_MODEL = 1024
DEPTH = 2
GRID_W = 64
WIN_ROWS = 8
WIN_COLS = 16
N_ATTN_HEADS = 8
HEAD_DIM = 64
D_ATTN = N_ATTN_HEADS * HEAD_DIM
N_FOURIER_GROUPS = 4
FOURIER_GROUP = 128
D_FOURIER = N_FOURIER_GROUPS * FOURIER_GROUP
D_IN = 3 * D_ATTN + D_FOURIER
D_FF = 2816
ALPHA = (2.0 * DEPTH) ** 0.25
LN_EPS = 1e-5
RMS_EPS = 1e-6
NEG_INF = -1e30

V7X_LANES = 128
V7X_MXU_DIM = 256
V7X_VMEM_BYTES = 64 * 1024 * 1024

F32 = jnp.float32
BF16 = jnp.bfloat16


def _cparams(vmem_bytes, n_axes):
    return pltpu.CompilerParams(
        dimension_semantics=("arbitrary",) * n_axes,
        vmem_limit_bytes=min(int(vmem_bytes), V7X_VMEM_BYTES - (2 << 20)),
    )


def _const_spec(shape):
    nd = len(shape)
    return pl.BlockSpec(shape, lambda *_: (0,) * nd, pipeline_mode=pl.Buffered(1))


def _layer_norm(z, g, b):
    mu = jnp.mean(z, axis=-1, keepdims=True)
    zc = z - mu
    var = jnp.mean(zc * zc, axis=-1, keepdims=True)
    return zc * lax.rsqrt(var + LN_EPS) * g + b


def _rms_norm(z, g):
    return z * lax.rsqrt(jnp.mean(z * z, axis=-1, keepdims=True) + RMS_EPS) * g


def _dot(a, b):
    return jnp.dot(a, b, preferred_element_type=F32)


TOK_TM = 512
FFN_FC = V7X_MXU_DIM
Q_SCALE = HEAD_DIM ** -0.5 * math.log2(math.e)


def _swiglu_ln(x, wg_ref, wu_ref, wd_ref, g_ref, b_ref, h_ref):
    xb = x.astype(BF16)
    for c in range(D_FF // FFN_FC):
        cols = slice(c * FFN_FC, (c + 1) * FFN_FC)
        gate = _dot(xb, wg_ref[:, cols])
        up = _dot(xb, wu_ref[:, cols])
        h_ref[:, cols] = (gate * jax.nn.sigmoid(gate) * up).astype(BF16)
    y = _dot(h_ref[...], wd_ref[...])
    return _layer_norm(ALPHA * x + 0.5 * y, g_ref[...], b_ref[...])


def _ffn_proj_kernel(x_ref, wg_ref, wu_ref, wd_ref, g_ref, b_ref, win_ref, cs_ref,
                     x1_ref, q_ref, k_ref, v_ref, a_ref, bb_ref, h_ref):
    x1 = _swiglu_ln(x_ref[...], wg_ref, wu_ref, wd_ref, g_ref, b_ref, h_ref)
    x1_ref[...] = x1
    xb = x1.astype(BF16)
    q_ref[...] = (_dot(xb, win_ref[:, 0:D_ATTN]) * Q_SCALE).astype(BF16)
    k_ref[...] = _dot(xb, win_ref[:, D_ATTN:2 * D_ATTN]).astype(BF16)
    v_ref[...] = _dot(xb, win_ref[:, 2 * D_ATTN:3 * D_ATTN]).astype(BF16)
    u = _dot(xb, win_ref[:, 3 * D_ATTN:]).astype(BF16)
    for grp in range(N_FOURIER_GROUPS):
        cols = slice(grp * FOURIER_GROUP, (grp + 1) * FOURIER_GROUP)
        ab = _dot(u[:, cols], cs_ref[...])
        a_ref[grp] = ab[:, :FOURIER_GROUP]
        bb_ref[grp] = ab[:, FOURIER_GROUP:]


def _mix_ffn_kernel(attn_ref, four_ref, x_ref, ga_ref, gf_ref, wo_ref, g2_ref, b2_ref,
                    wg_ref, wu_ref, wd_ref, g3_ref, b3_ref, o_ref, h_ref):
    ma = _rms_norm(attn_ref[...], ga_ref[...]).astype(BF16)
    mf = _rms_norm(four_ref[...], gf_ref[...]).astype(BF16)
    z = _dot(ma, wo_ref[0:D_ATTN, :]) + _dot(mf, wo_ref[D_ATTN:, :])
    x2 = _layer_norm(ALPHA * x_ref[...] + z, g2_ref[...], b2_ref[...])
    o_ref[...] = _swiglu_ln(x2, wg_ref, wu_ref, wd_ref, g3_ref, b3_ref, h_ref)


def _tok_spec(width):
    return pl.BlockSpec((TOK_TM, width), lambda i: (i, 0))


_FFN_WEIGHT_SHAPES = [(D_MODEL, D_FF), (D_MODEL, D_FF), (D_FF, D_MODEL), (1, D_MODEL), (1, D_MODEL)]
_FFN_VMEM = 3 * D_MODEL * D_FF * 2 + TOK_TM * D_FF * 2 + 8 * TOK_TM * D_MODEL * 4


def _ffn_proj(x, wg, wu, wd, g, b, w_in, cs):
    n = x.shape[0]
    half = jax.ShapeDtypeStruct((n, D_ATTN), BF16)
    four = jax.ShapeDtypeStruct((N_FOURIER_GROUPS, n, FOURIER_GROUP), F32)
    four_spec = pl.BlockSpec((N_FOURIER_GROUPS, TOK_TM, FOURIER_GROUP), lambda i: (0, i, 0))
    vmem = (_FFN_VMEM + D_MODEL * D_IN * 2
            + 2 * TOK_TM * (2 * D_MODEL * 4 + 3 * D_ATTN * 2 + 2 * D_FOURIER * 4)
            + 2 * TOK_TM * D_IN * 4)
    return pl.pallas_call(
        _ffn_proj_kernel,
        grid=(n // TOK_TM,),
        in_specs=([_tok_spec(D_MODEL)] + [_const_spec(s) for s in _FFN_WEIGHT_SHAPES]
                  + [_const_spec((D_MODEL, D_IN)),
                     _const_spec((FOURIER_GROUP, 2 * FOURIER_GROUP))]),
        out_specs=[_tok_spec(D_MODEL)] + [_tok_spec(D_ATTN)] * 3 + [four_spec] * 2,
        out_shape=[jax.ShapeDtypeStruct((n, D_MODEL), F32)] + [half] * 3 + [four] * 2,
        scratch_shapes=[pltpu.VMEM((TOK_TM, D_FF), BF16)],
        compiler_params=_cparams(vmem, 1),
        name="ffn_proj",
    )(x, wg, wu, wd, g, b, w_in, cs)


def _mix_ffn(attn, four, x, ga, gf, wo, g2, b2, wg, wu, wd, g3, b3):
    n = x.shape[0]
    vmem = (_FFN_VMEM + D_MODEL * D_MODEL * 2
            + 2 * TOK_TM * (2 * D_MODEL * 4 + D_ATTN * 4 + D_FOURIER * 4))
    return pl.pallas_call(
        _mix_ffn_kernel,
        grid=(n // TOK_TM,),
        in_specs=([_tok_spec(D_ATTN), _tok_spec(D_FOURIER), _tok_spec(D_MODEL),
                   _const_spec((1, D_ATTN)), _const_spec((1, D_FOURIER)),
                   _const_spec((D_MODEL, D_MODEL)), _const_spec((1, D_MODEL)),
                   _const_spec((1, D_MODEL))]
                  + [_const_spec(s) for s in _FFN_WEIGHT_SHAPES]),
        out_specs=_tok_spec(D_MODEL),
        out_shape=jax.ShapeDtypeStruct((n, D_MODEL), F32),
        scratch_shapes=[pltpu.VMEM((TOK_TM, D_FF), BF16)],
        compiler_params=_cparams(vmem, 1),
        name="mix_ffn",
    )(attn, four, x, ga, gf, wo, g2, b2, wg, wu, wd, g3, b3)


PAIR_Q = 2 * GRID_W
PAIR_KROWS = WIN_ROWS + 2
PAIR_K = PAIR_KROWS * GRID_W
ATT_PAIRS_PER_STEP = 8
ATT_PAIRS_PER_ITER = 4
N_TZ = 2 * WIN_ROWS
LOG2E = math.log2(math.e)


def _toeplitz_tables(rpb_l):
    cols = np.arange(GRID_W)
    c0 = np.clip(cols - WIN_COLS // 2, 0, GRID_W - WIN_COLS)
    in_win = (cols[None, :] >= c0[:, None]) & (cols[None, :] < c0[:, None] + WIN_COLS)
    dc_idx = np.clip(cols[None, :] - cols[:, None], -(WIN_COLS - 1), WIN_COLS - 1) + (WIN_COLS - 1)
    onehot = (dc_idx[None] == np.arange(2 * WIN_COLS - 1)[:, None, None]).astype(np.float32)
    tz = jnp.einsum("hdj,jqk->hdqk", rpb_l.astype(F32) * LOG2E, jnp.asarray(onehot),
                    precision=lax.Precision.HIGHEST)
    tz = jnp.where(in_win, tz, NEG_INF)
    tzp = jnp.pad(tz, ((0, 0), (1, 1), (0, 0), (0, 0)), constant_values=NEG_INF)
    left, right = tzp[:, :-1], tzp[:, 1:]
    masked = jnp.full_like(left, NEG_INF)
    return jnp.concatenate([jnp.concatenate([left, right], axis=-1),
                            jnp.concatenate([left, masked], axis=-1),
                            jnp.concatenate([masked, right], axis=-1)], axis=1)


def _nattn_kernel(q_ref, k_ref, v_ref, tz_ref, o_ref, *, n_rows):
    step = pl.program_id(1)
    lane = lax.broadcasted_iota(jnp.int32, (PAIR_Q, V7X_LANES), 1)
    first_head = lane < HEAD_DIM

    def pair_geometry(pl_idx):
        p = step * ATT_PAIRS_PER_STEP + pl_idx
        w = jnp.clip(2 * p - WIN_ROWS // 2, 0, n_rows - PAIR_KROWS)

        def tz_index(a, t):
            r = 2 * p + a
            r0 = jnp.clip(r - WIN_ROWS // 2, 0, n_rows - WIN_ROWS)
            kr = w + 2 * t
            ok0 = (kr >= r0) & (kr < r0 + WIN_ROWS)
            ok1 = (kr + 1 >= r0) & (kr + 1 < r0 + WIN_ROWS)
            i = jnp.clip(kr - r + WIN_ROWS, 0, N_TZ - 1)
            return jnp.where(ok0 & ok1, i,
                             jnp.where(ok0, N_TZ + i, jnp.where(ok1, 2 * N_TZ + i, N_TZ)))

        return dict(
            qrows=pl.ds(pl.multiple_of(pl_idx * PAIR_Q, PAIR_Q), PAIR_Q),
            krows=pl.ds(pl.multiple_of(w * GRID_W, V7X_LANES), PAIR_K),
            tz_idx=[[tz_index(a, t) for t in range(PAIR_KROWS // 2)] for a in (0, 1)])

    def scores(geo, hp):
        lanes = slice(hp * V7X_LANES, (hp + 1) * V7X_LANES)
        qp = q_ref[geo["qrows"], lanes]
        kp = k_ref[geo["krows"], lanes]
        zero = jnp.zeros_like(qp)
        qq = jnp.concatenate([jnp.where(first_head, qp, zero),
                              jnp.where(first_head, zero, qp)], axis=0)
        return lax.dot_general(qq, kp, (((1,), (1,)), ((), ())),
                               preferred_element_type=F32)

    def finish(geo, hp, s):
        lanes = slice(hp * V7X_LANES, (hp + 1) * V7X_LANES)
        probs = []
        denom = []
        for e in (0, 1):
            h = 2 * hp + e
            bias = jnp.concatenate(
                [jnp.concatenate([tz_ref[h, geo["tz_idx"][a][t]] for t in range(PAIR_KROWS // 2)],
                                 axis=1) for a in (0, 1)], axis=0)
            se = s[e * PAIR_Q:(e + 1) * PAIR_Q] + bias
            m = jnp.max(se, axis=-1, keepdims=True)
            pe = jnp.exp2(se - m)
            denom.append(jnp.sum(pe, axis=-1, keepdims=True))
            probs.append(pe.astype(BF16))
        o = _dot(jnp.concatenate(probs, axis=0), v_ref[geo["krows"], lanes])
        o0 = o[:PAIR_Q] / denom[0]
        o1 = o[PAIR_Q:] / denom[1]
        o_ref[geo["qrows"], lanes] = jnp.where(first_head, o0, o1)

    def body(it, carry):
        geos = [pair_geometry(it * ATT_PAIRS_PER_ITER + j) for j in range(ATT_PAIRS_PER_ITER)]
        units = [(geos[j], hp) for j in range(ATT_PAIRS_PER_ITER)
                 for hp in range(N_ATTN_HEADS // 2)]
        s_cur = scores(*units[0])
        for u in range(1, len(units)):
            s_next = scores(*units[u])
            finish(*units[u - 1], s_cur)
            s_cur = s_next
        finish(*units[-1], s_cur)
        return carry

    lax.fori_loop(0, ATT_PAIRS_PER_STEP // ATT_PAIRS_PER_ITER, body, 0)


def _nattn(q, k, v, tz2, batch, seq):
    n_rows = seq // GRID_W
    tq = ATT_PAIRS_PER_STEP * PAIR_Q
    steps = seq // tq
    k3 = k.reshape(batch, seq, D_ATTN)
    v3 = v.reshape(batch, seq, D_ATTN)
    vmem = (2 * 2 * seq * D_ATTN * 2
            + tz2.size * 4
            + 2 * tq * D_ATTN * (2 + 4)
            + (8 << 20))
    kv_spec = pl.BlockSpec((None, seq, D_ATTN), lambda b, i: (b, 0, 0))
    return pl.pallas_call(
        functools.partial(_nattn_kernel, n_rows=n_rows),
        grid=(batch, steps),
        in_specs=[
            pl.BlockSpec((tq, D_ATTN), lambda b, i: (b * steps + i, 0)),
            kv_spec,
            kv_spec,
            _const_spec(tz2.shape),
        ],
        out_specs=pl.BlockSpec((tq, D_ATTN), lambda b, i: (b * steps + i, 0)),
        out_shape=jax.ShapeDtypeStruct((batch * seq, D_ATTN), F32),
        compiler_params=_cparams(vmem, 2),
        name="nattn",
    )(q, k3, v3, tz2)


SEQ_N1 = 128
SEQ_N2 = 64
FN1_S2_PER_STEP = 8
FN2_K1_PER_STEP = 8
FN1_ROWS = SEQ_N1 * FN1_S2_PER_STEP


FN1_STEPS = SEQ_N2 // FN1_S2_PER_STEP
FN2_STEPS = SEQ_N1 // FN2_K1_PER_STEP


def _fnet_kernel(a_ref, b_ref, w1_ref, tc_ref, ts_ref, w2_ref, y_ref, pr_ref, pi_ref):
    j = pl.program_id(1)

    @pl.when(j < FN1_STEPS)
    def _stage1():
        flat = (N_FOURIER_GROUPS * FN1_ROWS, FOURIER_GROUP)
        a2, b2 = a_ref.reshape(*flat), b_ref.reshape(*flat)
        tcos = tc_ref[...]
        tsin = ts_ref[...]

        def rows(grp, jj):
            return pl.ds(grp * FN1_ROWS + jj, SEQ_N1, stride=FN1_S2_PER_STEP)

        def gather(ref2, jj):
            return jnp.concatenate([ref2[rows(grp, jj), :] for grp in range(N_FOURIER_GROUPS)],
                                   axis=1)

        for jj in range(FN1_S2_PER_STEP):
            x = jnp.concatenate([gather(a2, jj).astype(BF16), gather(b2, jj).astype(BF16)],
                                axis=0)
            t = _dot(w1_ref[...], x)
            tr = t[:SEQ_N1]
            ti = t[SEQ_N1:]
            c = tcos[:, jj:jj + 1]
            s = tsin[:, jj:jj + 1]
            pr = tr * c + ti * s
            pi = ti * c - tr * s
            for grp in range(N_FOURIER_GROUPS):
                cols = slice(grp * FOURIER_GROUP, (grp + 1) * FOURIER_GROUP)
                pr_ref[j, rows(grp, jj), :] = pr[:, cols]
                pi_ref[j, rows(grp, jj), :] = pi[:, cols]

    @pl.when(j >= FN1_STEPS)
    def _stage2():
        kb = j - FN1_STEPS

        def slab(ref, kk):
            def tile(grp, blk):
                row0 = pl.multiple_of(
                    grp * FN1_ROWS + (kb * FN2_K1_PER_STEP + kk) * FN1_S2_PER_STEP,
                    FN1_S2_PER_STEP)
                return ref[blk, pl.ds(row0, FN1_S2_PER_STEP), :]
            return jnp.concatenate(
                [jnp.concatenate([tile(grp, blk) for blk in range(FN1_STEPS)], axis=0)
                 for grp in range(N_FOURIER_GROUPS)], axis=1).astype(BF16)

        for kk in range(FN2_K1_PER_STEP):
            x = jnp.concatenate([slab(pr_ref, kk), slab(pi_ref, kk)], axis=0)
            y_ref[:, kk, :] = _dot(w2_ref[...], x)


def _fnet(a, b, w1, w2, tcos, tsin, batch, seq):
    shape5 = (N_FOURIER_GROUPS, batch, SEQ_N1, SEQ_N2, FOURIER_GROUP)
    a5 = a.reshape(shape5)
    b5 = b.reshape(shape5)
    last1 = FN1_STEPS - 1
    in_spec = pl.BlockSpec((N_FOURIER_GROUPS, None, SEQ_N1, FN1_S2_PER_STEP, FOURIER_GROUP),
                           lambda bb, j: (0, bb, 0, jnp.minimum(j, last1), 0))
    tw_spec = pl.BlockSpec((None, SEQ_N1, FN1_S2_PER_STEP),
                           lambda bb, j: (jnp.minimum(j, last1), 0, 0))
    out_spec = pl.BlockSpec((None, SEQ_N2, FN2_K1_PER_STEP, D_FOURIER),
                            lambda bb, j: (bb, 0, jnp.maximum(j - FN1_STEPS, 0), 0))
    scratch = pltpu.VMEM((FN1_STEPS, N_FOURIER_GROUPS * FN1_ROWS, FOURIER_GROUP), F32)
    blk_bytes = N_FOURIER_GROUPS * FN1_ROWS * FOURIER_GROUP * 4
    vmem = (2 * FN1_STEPS * blk_bytes
            + 4 * blk_bytes
            + 2 * SEQ_N2 * FN2_K1_PER_STEP * D_FOURIER * 4
            + (8 << 20))
    y = pl.pallas_call(
        _fnet_kernel,
        grid=(batch, FN1_STEPS + FN2_STEPS),
        in_specs=[in_spec, in_spec, _const_spec((2 * SEQ_N1, 2 * SEQ_N1)), tw_spec, tw_spec,
                  _const_spec((SEQ_N2, 2 * SEQ_N2))],
        out_specs=out_spec,
        out_shape=jax.ShapeDtypeStruct((batch, SEQ_N2, SEQ_N1, D_FOURIER), F32),
        scratch_shapes=[scratch, scratch],
        compiler_params=_cparams(vmem, 2),
        name="fnet",
    )(a5, b5, w1, tcos, tsin, w2)
    return y.reshape(batch * seq, D_FOURIER)


def _dft_constants():
    def cs(n):
        idx = np.arange(n)
        ang = 2.0 * np.pi * ((idx[:, None] * idx[None, :]) % n) / n
        return np.cos(ang), np.sin(ang)

    cc, sc = cs(FOURIER_GROUP)
    chan = np.concatenate([cc, sc], axis=1)
    c1, s1 = cs(SEQ_N1)
    w1 = np.block([[c1, -s1], [-s1, -c1]])
    c2, s2 = cs(SEQ_N2)
    norm = 1.0 / math.sqrt(SEQ_N1 * SEQ_N2 * FOURIER_GROUP)
    w2 = np.concatenate([c2, s2], axis=1) * norm
    k1 = np.arange(SEQ_N1)
    s2i = np.arange(SEQ_N2)
    ang = 2.0 * np.pi * (k1[:, None] * s2i[None, :]) / (SEQ_N1 * SEQ_N2)
    nblk = SEQ_N2 // FN1_S2_PER_STEP

    def blocked(t):
        return t.reshape(SEQ_N1, nblk, FN1_S2_PER_STEP).transpose(1, 0, 2)

    return (jnp.asarray(chan, F32).astype(BF16), jnp.asarray(w1, F32).astype(BF16),
            jnp.asarray(w2, F32).astype(BF16),
            jnp.asarray(blocked(np.cos(ang)), F32), jnp.asarray(blocked(np.sin(ang)), F32))


def kernel(x, ffn1_w_gate, ffn1_w_up, ffn1_w_down, ln1_g, ln1_b, w_in, rpb, g_attn, g_fourier,
           w_out, ln2_g, ln2_b, ffn2_w_gate, ffn2_w_up, ffn2_w_down, ln3_g, ln3_b):
    batch, seq, d = x.shape
    assert d == D_MODEL and seq == SEQ_N1 * SEQ_N2 and seq % (ATT_PAIRS_PER_STEP * PAIR_Q) == 0
    assert rpb.shape[0] == DEPTH
    chan, w1, w2, tcos, tsin = _dft_constants()

    def row(p, l):
        return p[l].reshape(1, -1).astype(F32)

    h = x.reshape(batch * seq, d).astype(F32)
    for l in range(DEPTH):
        h, q, k, v, a, b = _ffn_proj(
            h, ffn1_w_gate[l].astype(BF16), ffn1_w_up[l].astype(BF16),
            ffn1_w_down[l].astype(BF16), row(ln1_g, l), row(ln1_b, l),
            w_in[l].astype(BF16), chan)
        attn = _nattn(q, k, v, _toeplitz_tables(rpb[l]), batch, seq)
        four = _fnet(a, b, w1, w2, tcos, tsin, batch, seq)
        h = _mix_ffn(attn, four, h, row(g_attn, l), row(g_fourier, l), w_out[l].astype(BF16),
                     row(ln2_g, l), row(ln2_b, l),
                     ffn2_w_gate[l].astype(BF16), ffn2_w_up[l].astype(BF16),
                     ffn2_w_down[l].astype(BF16), row(ln3_g, l), row(ln3_b, l))
    return h.reshape(batch, seq, d).astype(x.dtype)
```

```python
import functools
import math

import jax
import jax.numpy as jnp
import numpy as np
from jax import lax
from jax.experimental import pallas as pl
from jax.experimental.pallas import tpu as pltpu

D_MODEL = 1024
DEPTH = 2
GRID_W = 64
WIN_ROWS = 8
WIN_COLS = 16
N_ATTN_HEADS = 8
HEAD_DIM = 64
D_ATTN = N_ATTN_HEADS * HEAD_DIM
N_FOURIER_GROUPS = 4
FOURIER_GROUP = 128
D_FOURIER = N_FOURIER_GROUPS * FOURIER_GROUP
D_IN = 3 * D_ATTN + D_FOURIER
D_FF = 2816
ALPHA = (2.0 * DEPTH) ** 0.25
LN_EPS = 1e-5
RMS_EPS = 1e-6
NEG_INF = -1e30

V7X_LANES = 128
V7X_MXU_DIM = 256
V7X_VMEM_BYTES = 64 * 1024 * 1024

F32 = jnp.float32
BF16 = jnp.bfloat16


def _cparams(vmem_bytes, n_axes):
    return pltpu.CompilerParams(
        dimension_semantics=("arbitrary",) * n_axes,
        vmem_limit_bytes=min(int(vmem_bytes), V7X_VMEM_BYTES - (2 << 20)),
    )


def _const_spec(shape):
    nd = len(shape)
    return pl.BlockSpec(shape, lambda *_: (0,) * nd, pipeline_mode=pl.Buffered(1))


def _layer_norm(z, g, b):
    mu = jnp.mean(z, axis=-1, keepdims=True)
    zc = z - mu
    var = jnp.mean(zc * zc, axis=-1, keepdims=True)
    return zc * lax.rsqrt(var + LN_EPS) * g + b


def _rms_norm(z, g):
    return z * lax.rsqrt(jnp.mean(z * z, axis=-1, keepdims=True) + RMS_EPS) * g


def _dot(a, b):
    return jnp.dot(a, b, preferred_element_type=F32)


TOK_TM = 512
FFN_FC = V7X_MXU_DIM
Q_SCALE = HEAD_DIM ** -0.5 * math.log2(math.e)


TOK_SUB = 256
N_SUB = TOK_TM // TOK_SUB


def _sub_rows(s):
    return slice(s * TOK_SUB, (s + 1) * TOK_SUB)


def _swiglu_ln(xs, wg_ref, wu_ref, wd_ref, g_ref, b_ref, h_ref):
    xbs = [x.astype(BF16) for x in xs]
    for c in range(D_FF // FFN_FC):
        cols = slice(c * FFN_FC, (c + 1) * FFN_FC)
        for s, xb in enumerate(xbs):
            gate = _dot(xb, wg_ref[:, cols])
            up = _dot(xb, wu_ref[:, cols])
            h_ref[s, :, cols] = (gate * jax.nn.sigmoid(gate) * up).astype(BF16)
    outs = []
    for s, x in enumerate(xs):
        y = _dot(h_ref[s], wd_ref[...])
        outs.append(_layer_norm(ALPHA * x + 0.5 * y, g_ref[...], b_ref[...]))
    return outs


def _ffn_proj_kernel(x_ref, wg_ref, wu_ref, wd_ref, g_ref, b_ref, win_ref, cs_ref,
                     x1_ref, q_ref, k_ref, v_ref, a_ref, bb_ref, h_ref):
    x1s = _swiglu_ln([x_ref[_sub_rows(s), :] for s in range(N_SUB)],
                     wg_ref, wu_ref, wd_ref, g_ref, b_ref, h_ref)
    for s, x1 in enumerate(x1s):
        rows = _sub_rows(s)
        x1_ref[rows, :] = x1
        xb = x1.astype(BF16)
        q_ref[rows, :] = (_dot(xb, win_ref[:, 0:D_ATTN]) * Q_SCALE).astype(BF16)
        k_ref[rows, :] = _dot(xb, win_ref[:, D_ATTN:2 * D_ATTN]).astype(BF16)
        v_ref[rows, :] = _dot(xb, win_ref[:, 2 * D_ATTN:3 * D_ATTN]).astype(BF16)
        u = _dot(xb, win_ref[:, 3 * D_ATTN:]).astype(BF16)
        for grp in range(N_FOURIER_GROUPS):
            cols = slice(grp * FOURIER_GROUP, (grp + 1) * FOURIER_GROUP)
            ab = _dot(u[:, cols], cs_ref[...])
            a_ref[grp, rows, :] = ab[:, :FOURIER_GROUP]
            bb_ref[grp, rows, :] = ab[:, FOURIER_GROUP:]


def _mix_ffn_kernel(attn_ref, four_ref, x_ref, ga_ref, gf_ref, wo_ref, g2_ref, b2_ref,
                    wg_ref, wu_ref, wd_ref, g3_ref, b3_ref, o_ref, h_ref):
    x2s = []
    for s in range(N_SUB):
        rows = _sub_rows(s)
        ma = _rms_norm(attn_ref[rows, :], ga_ref[...]).astype(BF16)
        mf = _rms_norm(four_ref[rows, :], gf_ref[...]).astype(BF16)
        z = _dot(ma, wo_ref[0:D_ATTN, :]) + _dot(mf, wo_ref[D_ATTN:, :])
        x2s.append(_layer_norm(ALPHA * x_ref[rows, :] + z, g2_ref[...], b2_ref[...]))
    outs = _swiglu_ln(x2s, wg_ref, wu_ref, wd_ref, g3_ref, b3_ref, h_ref)
    for s, out in enumerate(outs):
        o_ref[_sub_rows(s), :] = out


def _tok_spec(width):
    return pl.BlockSpec((TOK_TM, width), lambda i: (i, 0))


_FFN_WEIGHT_SHAPES = [(D_MODEL, D_FF), (D_MODEL, D_FF), (D_FF, D_MODEL), (1, D_MODEL), (1, D_MODEL)]
_FFN_VMEM = 3 * D_MODEL * D_FF * 2 + TOK_TM * D_FF * 2 + 8 * TOK_TM * D_MODEL * 4


def _ffn_proj(x, wg, wu, wd, g, b, w_in, cs):
    n = x.shape[0]
    half = jax.ShapeDtypeStruct((n, D_ATTN), BF16)
    four = jax.ShapeDtypeStruct((N_FOURIER_GROUPS, n, FOURIER_GROUP), F32)
    four_spec = pl.BlockSpec((N_FOURIER_GROUPS, TOK_TM, FOURIER_GROUP), lambda i: (0, i, 0))
    vmem = (_FFN_VMEM + D_MODEL * D_IN * 2
            + 2 * TOK_TM * (2 * D_MODEL * 4 + 3 * D_ATTN * 2 + 2 * D_FOURIER * 4)
            + 2 * TOK_TM * D_IN * 4)
    return pl.pallas_call(
        _ffn_proj_kernel,
        grid=(n // TOK_TM,),
        in_specs=([_tok_spec(D_MODEL)] + [_const_spec(s) for s in _FFN_WEIGHT_SHAPES]
                  + [_const_spec((D_MODEL, D_IN)),
                     _const_spec((FOURIER_GROUP, 2 * FOURIER_GROUP))]),
        out_specs=[_tok_spec(D_MODEL)] + [_tok_spec(D_ATTN)] * 3 + [four_spec] * 2,
        out_shape=[jax.ShapeDtypeStruct((n, D_MODEL), F32)] + [half] * 3 + [four] * 2,
        scratch_shapes=[pltpu.VMEM((N_SUB, TOK_SUB, D_FF), BF16)],
        compiler_params=_cparams(vmem, 1),
        name="ffn_proj",
    )(x, wg, wu, wd, g, b, w_in, cs)


def _mix_ffn(attn, four, x, ga, gf, wo, g2, b2, wg, wu, wd, g3, b3):
    n = x.shape[0]
    vmem = (_FFN_VMEM + D_MODEL * D_MODEL * 2
            + 2 * TOK_TM * (2 * D_MODEL * 4 + D_ATTN * 4 + D_FOURIER * 4))
    return pl.pallas_call(
        _mix_ffn_kernel,
        grid=(n // TOK_TM,),
        in_specs=([_tok_spec(D_ATTN), _tok_spec(D_FOURIER), _tok_spec(D_MODEL),
                   _const_spec((1, D_ATTN)), _const_spec((1, D_FOURIER)),
                   _const_spec((D_MODEL, D_MODEL)), _const_spec((1, D_MODEL)),
                   _const_spec((1, D_MODEL))]
                  + [_const_spec(s) for s in _FFN_WEIGHT_SHAPES]),
        out_specs=_tok_spec(D_MODEL),
        out_shape=jax.ShapeDtypeStruct((n, D_MODEL), F32),
        scratch_shapes=[pltpu.VMEM((N_SUB, TOK_SUB, D_FF), BF16)],
        compiler_params=_cparams(vmem, 1),
        name="mix_ffn",
    )(attn, four, x, ga, gf, wo, g2, b2, wg, wu, wd, g3, b3)


PAIR_Q = 2 * GRID_W
PAIR_KROWS = WIN_ROWS + 2
PAIR_K = PAIR_KROWS * GRID_W
ATT_PAIRS_PER_STEP = 16
ATT_PAIRS_PER_ITER = 4
N_TZ = 2 * WIN_ROWS
LOG2E = math.log2(math.e)
ATT_HALO = (WIN_ROWS // 2) * GRID_W
ATT_KV_WINDOW = ATT_PAIRS_PER_STEP * PAIR_Q + PAIR_K - PAIR_Q


def _kv_window_start(step, seq):
    tiles = jnp.clip(step * (ATT_PAIRS_PER_STEP * PAIR_Q // V7X_LANES) - ATT_HALO // V7X_LANES,
                     0, (seq - ATT_KV_WINDOW) // V7X_LANES)
    return tiles * V7X_LANES


def _toeplitz_tables(rpb_l):
    cols = np.arange(GRID_W)
    c0 = np.clip(cols - WIN_COLS // 2, 0, GRID_W - WIN_COLS)
    in_win = (cols[None, :] >= c0[:, None]) & (cols[None, :] < c0[:, None] + WIN_COLS)
    dc_idx = np.clip(cols[None, :] - cols[:, None], -(WIN_COLS - 1), WIN_COLS - 1) + (WIN_COLS - 1)
    onehot = (dc_idx[None] == np.arange(2 * WIN_COLS - 1)[:, None, None]).astype(np.float32)
    tz = jnp.einsum("hdj,jqk->hdqk", rpb_l.astype(F32) * LOG2E, jnp.asarray(onehot),
                    precision=lax.Precision.HIGHEST)
    tz = jnp.where(in_win, tz, NEG_INF)
    tzp = jnp.pad(tz, ((0, 0), (1, 1), (0, 0), (0, 0)), constant_values=NEG_INF)
    left, right = tzp[:, :-1], tzp[:, 1:]
    masked = jnp.full_like(left, NEG_INF)
    return jnp.concatenate([jnp.concatenate([left, right], axis=-1),
                            jnp.concatenate([left, masked], axis=-1),
                            jnp.concatenate([masked, right], axis=-1)], axis=1)


def _nattn_kernel(q_ref, k_ref, v_ref, tz_ref, o_ref, s_ref, *, n_rows):
    step = pl.program_id(1)
    kv_start = _kv_window_start(step, n_rows * GRID_W)
    lane = lax.broadcasted_iota(jnp.int32, (PAIR_Q, V7X_LANES), 1)
    first_head = lane < HEAD_DIM

    def pair_geometry(pl_idx):
        p = step * ATT_PAIRS_PER_STEP + pl_idx
        w = jnp.clip(2 * p - WIN_ROWS // 2, 0, n_rows - PAIR_KROWS)

        def tz_index(a, t):
            r = 2 * p + a
            r0 = jnp.clip(r - WIN_ROWS // 2, 0, n_rows - WIN_ROWS)
            kr = w + 2 * t
            ok0 = (kr >= r0) & (kr < r0 + WIN_ROWS)
            ok1 = (kr + 1 >= r0) & (kr + 1 < r0 + WIN_ROWS)
            i = jnp.clip(kr - r + WIN_ROWS, 0, N_TZ - 1)
            return jnp.where(ok0 & ok1, i,
                             jnp.where(ok0, N_TZ + i, jnp.where(ok1, 2 * N_TZ + i, N_TZ)))

        return dict(
            qrows=pl.ds(pl.multiple_of(pl_idx * PAIR_Q, PAIR_Q), PAIR_Q),
            krows=pl.ds(pl.multiple_of(w * GRID_W - kv_start, V7X_LANES), PAIR_K),
            tz_idx=[[tz_index(a, t) for t in range(PAIR_KROWS // 2)] for a in (0, 1)])

    def scores(geo, hp):
        lanes = slice(hp * V7X_LANES, (hp + 1) * V7X_LANES)
        qp = q_ref[geo["qrows"], lanes]
        kp = k_ref[geo["krows"], lanes]
        zero = jnp.zeros_like(qp)
        qq = jnp.concatenate([jnp.where(first_head, qp, zero),
                              jnp.where(first_head, zero, qp)], axis=0)
        return lax.dot_general(qq, kp, (((1,), (1,)), ((), ())),
                               preferred_element_type=F32)

    def finish(geo, hp, s):
        lanes = slice(hp * V7X_LANES, (hp + 1) * V7X_LANES)
        probs = []
        denom = []
        for e in (0, 1):
            h = 2 * hp + e
            bias = jnp.concatenate(
                [jnp.concatenate([tz_ref[h, geo["tz_idx"][a][t]] for t in range(PAIR_KROWS // 2)],
                                 axis=1) for a in (0, 1)], axis=0)
            se = s[e * PAIR_Q:(e + 1) * PAIR_Q] + bias
            m = jnp.max(se, axis=-1, keepdims=True)
            pe = jnp.exp2(se - m)
            denom.append(jnp.sum(pe, axis=-1, keepdims=True))
            probs.append(pe.astype(BF16))
        o = _dot(jnp.concatenate(probs, axis=0), v_ref[geo["krows"], lanes])
        o0 = o[:PAIR_Q] / denom[0]
        o1 = o[PAIR_Q:] / denom[1]
        o_ref[geo["qrows"], lanes] = jnp.where(first_head, o0, o1)

    def body(it, carry):
        geos = [pair_geometry(it * ATT_PAIRS_PER_ITER + j) for j in range(ATT_PAIRS_PER_ITER)]
        units = [(geos[j], hp) for j in range(ATT_PAIRS_PER_ITER)
                 for hp in range(N_ATTN_HEADS // 2)]
        s_cur = s_ref[...]
        for u in range(1, len(units)):
            s_next = scores(*units[u])
            finish(*units[u - 1], s_cur)
            s_cur = s_next
        nxt = jnp.minimum((it + 1) * ATT_PAIRS_PER_ITER, ATT_PAIRS_PER_STEP - 1)
        s_ref[...] = scores(pair_geometry(nxt), 0)
        finish(*units[-1], s_cur)
        return carry

    s_ref[...] = scores(pair_geometry(0), 0)
    lax.fori_loop(0, ATT_PAIRS_PER_STEP // ATT_PAIRS_PER_ITER, body, 0)


def _nattn(q, k, v, tz2, batch, seq):
    n_rows = seq // GRID_W
    tq = ATT_PAIRS_PER_STEP * PAIR_Q
    steps = seq // tq
    k3 = k.reshape(batch, seq, D_ATTN)
    v3 = v.reshape(batch, seq, D_ATTN)
    vmem = (2 * 2 * ATT_KV_WINDOW * D_ATTN * 2
            + tz2.size * 4
            + 2 * tq * D_ATTN * (2 + 4)
            + (8 << 20))
    kv_spec = pl.BlockSpec(
        (None, pl.Element(ATT_KV_WINDOW), pl.Element(D_ATTN)),
        lambda b, i: (b, _kv_window_start(i, seq), 0))
    return pl.pallas_call(
        functools.partial(_nattn_kernel, n_rows=n_rows),
        grid=(batch, steps),
        in_specs=[
            pl.BlockSpec((tq, D_ATTN), lambda b, i: (b * steps + i, 0)),
            kv_spec,
            kv_spec,
            _const_spec(tz2.shape),
        ],
        out_specs=pl.BlockSpec((tq, D_ATTN), lambda b, i: (b * steps + i, 0)),
        out_shape=jax.ShapeDtypeStruct((batch * seq, D_ATTN), F32),
        scratch_shapes=[pltpu.VMEM((2 * PAIR_Q, PAIR_K), F32)],
        compiler_params=_cparams(vmem, 2),
        name="nattn",
    )(q, k3, v3, tz2)


SEQ_N1 = 128
SEQ_N2 = 64
FN1_S2_PER_STEP = 8
FN2_K1_PER_STEP = 8
FN1_ROWS = SEQ_N1 * FN1_S2_PER_STEP


FN1_STEPS = SEQ_N2 // FN1_S2_PER_STEP
FN2_STEPS = SEQ_N1 // FN2_K1_PER_STEP


def _fnet_kernel(a_ref, b_ref, w1_ref, tc_ref, ts_ref, w2_ref, y_ref, pr_ref, pi_ref):
    j = pl.program_id(1)

    @pl.when(j < FN1_STEPS)
    def _stage1():
        flat = (N_FOURIER_GROUPS * FN1_ROWS, FOURIER_GROUP)
        a2, b2 = a_ref.reshape(*flat), b_ref.reshape(*flat)
        tcos = tc_ref[...]
        tsin = ts_ref[...]

        def rows(grp, jj):
            return pl.ds(grp * FN1_ROWS + jj, SEQ_N1, stride=FN1_S2_PER_STEP)

        def gather(ref2, jj):
            return jnp.concatenate([ref2[rows(grp, jj), :] for grp in range(N_FOURIER_GROUPS)],
                                   axis=1)

        for jj in range(FN1_S2_PER_STEP):
            x = jnp.concatenate([gather(a2, jj).astype(BF16), gather(b2, jj).astype(BF16)],
                                axis=0)
            t = _dot(w1_ref[...], x)
            tr = t[:SEQ_N1]
            ti = t[SEQ_N1:]
            c = tcos[:, jj:jj + 1]
            s = tsin[:, jj:jj + 1]
            pr = tr * c + ti * s
            pi = ti * c - tr * s
            for grp in range(N_FOURIER_GROUPS):
                cols = slice(grp * FOURIER_GROUP, (grp + 1) * FOURIER_GROUP)
                pr_ref[j, rows(grp, jj), :] = pr[:, cols]
                pi_ref[j, rows(grp, jj), :] = pi[:, cols]

    @pl.when(j >= FN1_STEPS)
    def _stage2():
        kb = j - FN1_STEPS

        def slab(ref, kk):
            def tile(grp, blk):
                row0 = pl.multiple_of(
                    grp * FN1_ROWS + (kb * FN2_K1_PER_STEP + kk) * FN1_S2_PER_STEP,
                    FN1_S2_PER_STEP)
                return ref[blk, pl.ds(row0, FN1_S2_PER_STEP), :]
            return jnp.concatenate(
                [jnp.concatenate([tile(grp, blk) for blk in range(FN1_STEPS)], axis=0)
                 for grp in range(N_FOURIER_GROUPS)], axis=1).astype(BF16)

        for kk in range(FN2_K1_PER_STEP):
            x = jnp.concatenate([slab(pr_ref, kk), slab(pi_ref, kk)], axis=0)
            y_ref[:, kk, :] = _dot(w2_ref[...], x)


def _fnet(a, b, w1, w2, tcos, tsin, batch, seq):
    shape5 = (N_FOURIER_GROUPS, batch, SEQ_N1, SEQ_N2, FOURIER_GROUP)
    a5 = a.reshape(shape5)
    b5 = b.reshape(shape5)
    last1 = FN1_STEPS - 1
    in_spec = pl.BlockSpec((N_FOURIER_GROUPS, None, SEQ_N1, FN1_S2_PER_STEP, FOURIER_GROUP),
                           lambda bb, j: (0, bb, 0, jnp.minimum(j, last1), 0))
    tw_spec = pl.BlockSpec((None, SEQ_N1, FN1_S2_PER_STEP),
                           lambda bb, j: (jnp.minimum(j, last1), 0, 0))
    out_spec = pl.BlockSpec((None, SEQ_N2, FN2_K1_PER_STEP, D_FOURIER),
                            lambda bb, j: (bb, 0, jnp.maximum(j - FN1_STEPS, 0), 0))
    scratch = pltpu.VMEM((FN1_STEPS, N_FOURIER_GROUPS * FN1_ROWS, FOURIER_GROUP), F32)
    blk_bytes = N_FOURIER_GROUPS * FN1_ROWS * FOURIER_GROUP * 4
    vmem = (2 * FN1_STEPS * blk_bytes
            + 4 * blk_bytes
            + 2 * SEQ_N2 * FN2_K1_PER_STEP * D_FOURIER * 4
            + (8 << 20))
    y = pl.pallas_call(
        _fnet_kernel,
        grid=(batch, FN1_STEPS + FN2_STEPS),
        in_specs=[in_spec, in_spec, _const_spec((2 * SEQ_N1, 2 * SEQ_N1)), tw_spec, tw_spec,
                  _const_spec((SEQ_N2, 2 * SEQ_N2))],
        out_specs=out_spec,
        out_shape=jax.ShapeDtypeStruct((batch, SEQ_N2, SEQ_N1, D_FOURIER), F32),
        scratch_shapes=[scratch, scratch],
        compiler_params=_cparams(vmem, 2),
        name="fnet",
    )(a5, b5, w1, tcos, tsin, w2)
    return y.reshape(batch * seq, D_FOURIER)


def _dft_constants():
    def cs(n):
        idx = np.arange(n)
        ang = 2.0 * np.pi * ((idx[:, None] * idx[None, :]) % n) / n
        return np.cos(ang), np.sin(ang)

    cc, sc = cs(FOURIER_GROUP)
    chan = np.concatenate([cc, sc], axis=1)
    c1, s1 = cs(SEQ_N1)
    w1 = np.block([[c1, -s1], [-s1, -c1]])
    c2, s2 = cs(SEQ_N2)
    norm = 1.0 / math.sqrt(SEQ_N1 * SEQ_N2 * FOURIER_GROUP)
    w2 = np.concatenate([c2, s2], axis=1) * norm
    k1 = np.arange(SEQ_N1)
    s2i = np.arange(SEQ_N2)
    ang = 2.0 * np.pi * (k1[:, None] * s2i[None, :]) / (SEQ_N1 * SEQ_N2)
    nblk = SEQ_N2 // FN1_S2_PER_STEP

    def blocked(t):
        return t.reshape(SEQ_N1, nblk, FN1_S2_PER_STEP).transpose(1, 0, 2)

    return (jnp.asarray(chan, F32).astype(BF16), jnp.asarray(w1, F32).astype(BF16),
            jnp.asarray(w2, F32).astype(BF16),
            jnp.asarray(blocked(np.cos(ang)), F32), jnp.asarray(blocked(np.sin(ang)), F32))


def kernel(x, ffn1_w_gate, ffn1_w_up, ffn1_w_down, ln1_g, ln1_b, w_in, rpb, g_attn, g_fourier,
           w_out, ln2_g, ln2_b, ffn2_w_gate, ffn2_w_up, ffn2_w_down, ln3_g, ln3_b):
    batch, seq, d = x.shape
    assert d == D_MODEL and seq == SEQ_N1 * SEQ_N2 and seq % (ATT_PAIRS_PER_STEP * PAIR_Q) == 0
    assert rpb.shape[0] == DEPTH
    chan, w1, w2, tcos, tsin = _dft_constants()

    def row(p, l):
        return p[l].reshape(1, -1).astype(F32)

    h = x.reshape(batch * seq, d).astype(F32)
    for l in range(DEPTH):
        h, q, k, v, a, b = _ffn_proj(
            h, ffn1_w_gate[l].astype(BF16), ffn1_w_up[l].astype(BF16),
            ffn1_w_down[l].astype(BF16), row(ln1_g, l), row(ln1_b, l),
            w_in[l].astype(BF16), chan)
        attn = _nattn(q, k, v, _toeplitz_tables(rpb[l]), batch, seq)
        four = _fnet(a, b, w1, w2, tcos, tsin, batch, seq)
        h = _mix_ffn(attn, four, h, row(g_attn, l), row(g_fourier, l), w_out[l].astype(BF16),
                     row(ln2_g, l), row(ln2_b, l),
                     ffn2_w_gate[l].astype(BF16), ffn2_w_up[l].astype(BF16),
                     ffn2_w_down[l].astype(BF16), row(ln3_g, l), row(ln3_b, l))
    return h.reshape(batch, seq, d).astype(x.dtype)
```

```python
import functools
import math

import jax
import jax.numpy as jnp
import numpy as np
from jax import lax
from jax.experimental import pallas as pl
from jax.experimental.pallas import tpu as pltpu

D_MODEL = 1024
DEPTH = 2
GRID_W = 64
WIN_ROWS = 8
WIN_COLS = 16
N_ATTN_HEADS = 8
HEAD_DIM = 64
D_ATTN = N_ATTN_HEADS * HEAD_DIM
N_FOURIER_GROUPS = 4
FOURIER_GROUP = 128
D_FOURIER = N_FOURIER_GROUPS * FOURIER_GROUP
D_IN = 3 * D_ATTN + D_FOURIER
D_FF = 2816
ALPHA = (2.0 * DEPTH) ** 0.25
LN_EPS = 1e-5
RMS_EPS = 1e-6
NEG_INF = -1e30

V7X_LANES = 128
V7X_MXU_DIM = 256
V7X_VMEM_BYTES = 64 * 1024 * 1024

F32 = jnp.float32
BF16 = jnp.bfloat16


def _cparams(vmem_bytes, n_axes):
    return pltpu.CompilerParams(
        dimension_semantics=("arbitrary",) * n_axes,
        vmem_limit_bytes=min(int(vmem_bytes), V7X_VMEM_BYTES - (2 << 20)),
    )


def _const_spec(shape):
    nd = len(shape)
    return pl.BlockSpec(shape, lambda *_: (0,) * nd, pipeline_mode=pl.Buffered(1))


def _layer_norm(z, g, b):
    mu = jnp.mean(z, axis=-1, keepdims=True)
    zc = z - mu
    var = jnp.mean(zc * zc, axis=-1, keepdims=True)
    return zc * lax.rsqrt(var + LN_EPS) * g + b


def _rms_norm(z, g):
    return z * lax.rsqrt(jnp.mean(z * z, axis=-1, keepdims=True) + RMS_EPS) * g


def _dot(a, b):
    return jnp.dot(a, b, preferred_element_type=F32)


TOK_TM = 512
FFN_FC = V7X_MXU_DIM
Q_SCALE = HEAD_DIM ** -0.5 * math.log2(math.e)


TOK_SUB = 256
N_SUB = TOK_TM // TOK_SUB


def _sub_rows(s):
    return slice(s * TOK_SUB, (s + 1) * TOK_SUB)


def _swiglu_ln(xs, wg_ref, wu_ref, wd_ref, g_ref, b_ref, h_ref):
    xbs = [x.astype(BF16) for x in xs]
    for c in range(D_FF // FFN_FC):
        cols = slice(c * FFN_FC, (c + 1) * FFN_FC)
        for s, xb in enumerate(xbs):
            gate = _dot(xb, wg_ref[:, cols])
            up = _dot(xb, wu_ref[:, cols])
            h_ref[s, :, cols] = (gate * jax.nn.sigmoid(gate) * up).astype(BF16)
    outs = []
    for s, x in enumerate(xs):
        y = _dot(h_ref[s], wd_ref[...])
        outs.append(_layer_norm(ALPHA * x + 0.5 * y, g_ref[...], b_ref[...]))
    return outs


def _ffn_proj_kernel(x_ref, wg_ref, wu_ref, wd_ref, g_ref, b_ref, win_ref, cs_ref,
                     x1_ref, q_ref, k_ref, v_ref, a_ref, bb_ref, h_ref):
    x1s = _swiglu_ln([x_ref[_sub_rows(s), :] for s in range(N_SUB)],
                     wg_ref, wu_ref, wd_ref, g_ref, b_ref, h_ref)
    for s, x1 in enumerate(x1s):
        rows = _sub_rows(s)
        x1_ref[rows, :] = x1
        xb = x1.astype(BF16)
        q_ref[rows, :] = (_dot(xb, win_ref[:, 0:D_ATTN]) * Q_SCALE).astype(BF16)
        k_ref[rows, :] = _dot(xb, win_ref[:, D_ATTN:2 * D_ATTN]).astype(BF16)
        v_ref[rows, :] = _dot(xb, win_ref[:, 2 * D_ATTN:3 * D_ATTN]).astype(BF16)
        u = _dot(xb, win_ref[:, 3 * D_ATTN:]).astype(BF16)
        for grp in range(N_FOURIER_GROUPS):
            cols = slice(grp * FOURIER_GROUP, (grp + 1) * FOURIER_GROUP)
            ab = _dot(u[:, cols], cs_ref[...])
            a_ref[grp, rows, :] = ab[:, :FOURIER_GROUP]
            bb_ref[grp, rows, :] = ab[:, FOURIER_GROUP:]


def _mix_ffn_kernel(attn_ref, four_ref, x_ref, ga_ref, gf_ref, wo_ref, g2_ref, b2_ref,
                    wg_ref, wu_ref, wd_ref, g3_ref, b3_ref, o_ref, h_ref):
    x2s = []
    for s in range(N_SUB):
        rows = _sub_rows(s)
        ma = _rms_norm(attn_ref[rows, :], ga_ref[...]).astype(BF16)
        mf = _rms_norm(four_ref[rows, :], gf_ref[...]).astype(BF16)
        z = _dot(ma, wo_ref[0:D_ATTN, :]) + _dot(mf, wo_ref[D_ATTN:, :])
        x2s.append(_layer_norm(ALPHA * x_ref[rows, :] + z, g2_ref[...], b2_ref[...]))
    outs = _swiglu_ln(x2s, wg_ref, wu_ref, wd_ref, g3_ref, b3_ref, h_ref)
    for s, out in enumerate(outs):
        o_ref[_sub_rows(s), :] = out


def _tok_spec(width):
    return pl.BlockSpec((TOK_TM, width), lambda i: (i, 0))


def _layer_spec(layer, rows, cols):
    return pl.BlockSpec((None, rows, cols), lambda i: (layer, 0, 0), pipeline_mode=pl.Buffered(1))


def _ffn_weight_specs(layer):
    return [_layer_spec(layer, D_MODEL, D_FF), _layer_spec(layer, D_MODEL, D_FF),
            _layer_spec(layer, D_FF, D_MODEL), _const_spec((1, D_MODEL)), _const_spec((1, D_MODEL))]


_FFN_VMEM = 3 * D_MODEL * D_FF * 2 + TOK_TM * D_FF * 2 + 8 * TOK_TM * D_MODEL * 4

CAST_STEPS = 16


def _cast_kernel(*refs):
    n = len(refs) // 2
    for src, dst in zip(refs[:n], refs[n:]):
        dst[...] = src[...].astype(BF16)


def _cast_weights(weights):
    flat = [w.reshape(-1, w.shape[-1]) for w in weights]
    specs = [pl.BlockSpec((f.shape[0] // CAST_STEPS, f.shape[1]), lambda i: (i, 0)) for f in flat]
    vmem = 2 * sum(f.size // CAST_STEPS * 6 for f in flat) + (4 << 20)
    outs = pl.pallas_call(
        _cast_kernel,
        grid=(CAST_STEPS,),
        in_specs=specs,
        out_specs=specs,
        out_shape=[jax.ShapeDtypeStruct(f.shape, BF16) for f in flat],
        compiler_params=_cparams(vmem, 1),
        name="cast_weights",
    )(*flat)
    return [o.reshape(w.shape) for o, w in zip(outs, weights)]


def _ffn_proj(layer, x, wg, wu, wd, g, b, w_in, cs):
    n = x.shape[0]
    half = jax.ShapeDtypeStruct((n, D_ATTN), BF16)
    four = jax.ShapeDtypeStruct((N_FOURIER_GROUPS, n, FOURIER_GROUP), F32)
    four_spec = pl.BlockSpec((N_FOURIER_GROUPS, TOK_TM, FOURIER_GROUP), lambda i: (0, i, 0))
    vmem = (_FFN_VMEM + D_MODEL * D_IN * 2
            + 2 * TOK_TM * (2 * D_MODEL * 4 + 3 * D_ATTN * 2 + 2 * D_FOURIER * 4)
            + 2 * TOK_TM * D_IN * 4)
    return pl.pallas_call(
        _ffn_proj_kernel,
        grid=(n // TOK_TM,),
        in_specs=([_tok_spec(D_MODEL)] + _ffn_weight_specs(layer)
                  + [_layer_spec(layer, D_MODEL, D_IN),
                     _const_spec((FOURIER_GROUP, 2 * FOURIER_GROUP))]),
        out_specs=[_tok_spec(D_MODEL)] + [_tok_spec(D_ATTN)] * 3 + [four_spec] * 2,
        out_shape=[jax.ShapeDtypeStruct((n, D_MODEL), F32)] + [half] * 3 + [four] * 2,
        scratch_shapes=[pltpu.VMEM((N_SUB, TOK_SUB, D_FF), BF16)],
        compiler_params=_cparams(vmem, 1),
        name="ffn_proj",
    )(x, wg, wu, wd, g, b, w_in, cs)


def _mix_ffn(layer, attn, four, x, ga, gf, wo, g2, b2, wg, wu, wd, g3, b3):
    n = x.shape[0]
    vmem = (_FFN_VMEM + D_MODEL * D_MODEL * 2
            + 2 * TOK_TM * (2 * D_MODEL * 4 + D_ATTN * 4 + D_FOURIER * 4))
    return pl.pallas_call(
        _mix_ffn_kernel,
        grid=(n // TOK_TM,),
        in_specs=([_tok_spec(D_ATTN), _tok_spec(D_FOURIER), _tok_spec(D_MODEL),
                   _const_spec((1, D_ATTN)), _const_spec((1, D_FOURIER)),
                   _layer_spec(layer, D_MODEL, D_MODEL), _const_spec((1, D_MODEL)),
                   _const_spec((1, D_MODEL))]
                  + _ffn_weight_specs(layer)),
        out_specs=_tok_spec(D_MODEL),
        out_shape=jax.ShapeDtypeStruct((n, D_MODEL), F32),
        scratch_shapes=[pltpu.VMEM((N_SUB, TOK_SUB, D_FF), BF16)],
        compiler_params=_cparams(vmem, 1),
        name="mix_ffn",
    )(attn, four, x, ga, gf, wo, g2, b2, wg, wu, wd, g3, b3)


PAIR_Q = 2 * GRID_W
PAIR_KROWS = WIN_ROWS + 2
PAIR_K = PAIR_KROWS * GRID_W
ATT_PAIRS_PER_STEP = 16
ATT_PAIRS_PER_ITER = 4
N_TZ = 2 * WIN_ROWS
LOG2E = math.log2(math.e)
ATT_HALO = (WIN_ROWS // 2) * GRID_W
ATT_KV_WINDOW = ATT_PAIRS_PER_STEP * PAIR_Q + PAIR_K - PAIR_Q


def _kv_window_start(step, seq):
    tiles = jnp.clip(step * (ATT_PAIRS_PER_STEP * PAIR_Q // V7X_LANES) - ATT_HALO // V7X_LANES,
                     0, (seq - ATT_KV_WINDOW) // V7X_LANES)
    return tiles * V7X_LANES


def _toeplitz_tables(rpb_l):
    cols = np.arange(GRID_W)
    c0 = np.clip(cols - WIN_COLS // 2, 0, GRID_W - WIN_COLS)
    in_win = (cols[None, :] >= c0[:, None]) & (cols[None, :] < c0[:, None] + WIN_COLS)
    dc_idx = np.clip(cols[None, :] - cols[:, None], -(WIN_COLS - 1), WIN_COLS - 1) + (WIN_COLS - 1)
    onehot = (dc_idx[None] == np.arange(2 * WIN_COLS - 1)[:, None, None]).astype(np.float32)
    tz = jnp.einsum("hdj,jqk->hdqk", rpb_l.astype(F32) * LOG2E, jnp.asarray(onehot),
                    precision=lax.Precision.HIGHEST)
    tz = jnp.where(in_win, tz, NEG_INF)
    tzp = jnp.pad(tz, ((0, 0), (1, 1), (0, 0), (0, 0)), constant_values=NEG_INF)
    left, right = tzp[:, :-1], tzp[:, 1:]
    masked = jnp.full_like(left, NEG_INF)
    return jnp.concatenate([jnp.concatenate([left, right], axis=-1),
                            jnp.concatenate([left, masked], axis=-1),
                            jnp.concatenate([masked, right], axis=-1)], axis=1)


def _nattn_kernel(q_ref, k_ref, v_ref, tz_ref, o_ref, s_ref, *, n_rows):
    step = pl.program_id(1)
    kv_start = _kv_window_start(step, n_rows * GRID_W)
    lane = lax.broadcasted_iota(jnp.int32, (PAIR_Q, V7X_LANES), 1)
    first_head = lane < HEAD_DIM
    ones_block = jnp.ones((PAIR_K, V7X_LANES), BF16)

    def pair_geometry(pl_idx):
        p = step * ATT_PAIRS_PER_STEP + pl_idx
        w = jnp.clip(2 * p - WIN_ROWS // 2, 0, n_rows - PAIR_KROWS)

        def tz_index(a, t):
            r = 2 * p + a
            r0 = jnp.clip(r - WIN_ROWS // 2, 0, n_rows - WIN_ROWS)
            kr = w + 2 * t
            ok0 = (kr >= r0) & (kr < r0 + WIN_ROWS)
            ok1 = (kr + 1 >= r0) & (kr + 1 < r0 + WIN_ROWS)
            i = jnp.clip(kr - r + WIN_ROWS, 0, N_TZ - 1)
            return jnp.where(ok0 & ok1, i,
                             jnp.where(ok0, N_TZ + i, jnp.where(ok1, 2 * N_TZ + i, N_TZ)))

        return dict(
            qrows=pl.ds(pl.multiple_of(pl_idx * PAIR_Q, PAIR_Q), PAIR_Q),
            krows=pl.ds(pl.multiple_of(w * GRID_W - kv_start, V7X_LANES), PAIR_K),
            tz_idx=[[tz_index(a, t) for t in range(PAIR_KROWS // 2)] for a in (0, 1)])

    def scores(geo, hp):
        lanes = slice(hp * V7X_LANES, (hp + 1) * V7X_LANES)
        qp = q_ref[geo["qrows"], lanes]
        kp = k_ref[geo["krows"], lanes]
        zero = jnp.zeros_like(qp)
        qq = jnp.concatenate([jnp.where(first_head, qp, zero),
                              jnp.where(first_head, zero, qp)], axis=0)
        return lax.dot_general(qq, kp, (((1,), (1,)), ((), ())),
                               preferred_element_type=F32)

    def finish(geo, hp, s):
        lanes = slice(hp * V7X_LANES, (hp + 1) * V7X_LANES)
        probs = []
        for e in (0, 1):
            h = 2 * hp + e
            bias = jnp.concatenate(
                [jnp.concatenate([tz_ref[h, geo["tz_idx"][a][t]] for t in range(PAIR_KROWS // 2)],
                                 axis=1) for a in (0, 1)], axis=0)
            se = s[e * PAIR_Q:(e + 1) * PAIR_Q] + bias
            m = jnp.max(se, axis=-1, keepdims=True)
            probs.append(jnp.exp2(se - m).astype(BF16))
        v_ext = jnp.concatenate([v_ref[geo["krows"], lanes], ones_block], axis=1)
        o = _dot(jnp.concatenate(probs, axis=0), v_ext)
        o = o[:, :V7X_LANES] / o[:, V7X_LANES:]
        o_ref[geo["qrows"], lanes] = jnp.where(first_head, o[:PAIR_Q], o[PAIR_Q:])

    def body(it, carry):
        geos = [pair_geometry(it * ATT_PAIRS_PER_ITER + j) for j in range(ATT_PAIRS_PER_ITER)]
        units = [(geos[j], hp) for j in range(ATT_PAIRS_PER_ITER)
                 for hp in range(N_ATTN_HEADS // 2)]
        s_cur = s_ref[...]
        for u in range(1, len(units)):
            s_next = scores(*units[u])
            finish(*units[u - 1], s_cur)
            s_cur = s_next
        nxt = jnp.minimum((it + 1) * ATT_PAIRS_PER_ITER, ATT_PAIRS_PER_STEP - 1)
        s_ref[...] = scores(pair_geometry(nxt), 0)
        finish(*units[-1], s_cur)
        return carry

    s_ref[...] = scores(pair_geometry(0), 0)
    lax.fori_loop(0, ATT_PAIRS_PER_STEP // ATT_PAIRS_PER_ITER, body, 0)


def _nattn(q, k, v, tz2, batch, seq):
    n_rows = seq // GRID_W
    tq = ATT_PAIRS_PER_STEP * PAIR_Q
    steps = seq // tq
    k3 = k.reshape(batch, seq, D_ATTN)
    v3 = v.reshape(batch, seq, D_ATTN)
    vmem = (2 * 2 * ATT_KV_WINDOW * D_ATTN * 2
            + tz2.size * 4
            + 2 * tq * D_ATTN * (2 + 4)
            + (8 << 20))
    kv_spec = pl.BlockSpec(
        (None, pl.Element(ATT_KV_WINDOW), pl.Element(D_ATTN)),
        lambda b, i: (b, _kv_window_start(i, seq), 0))
    return pl.pallas_call(
        functools.partial(_nattn_kernel, n_rows=n_rows),
        grid=(batch, steps),
        in_specs=[
            pl.BlockSpec((tq, D_ATTN), lambda b, i: (b * steps + i, 0)),
            kv_spec,
            kv_spec,
            _const_spec(tz2.shape),
        ],
        out_specs=pl.BlockSpec((tq, D_ATTN), lambda b, i: (b * steps + i, 0)),
        out_shape=jax.ShapeDtypeStruct((batch * seq, D_ATTN), F32),
        scratch_shapes=[pltpu.VMEM((2 * PAIR_Q, PAIR_K), F32)],
        compiler_params=_cparams(vmem, 2),
        name="nattn",
    )(q, k3, v3, tz2)


SEQ_N1 = 128
SEQ_N2 = 64
FN1_S2_PER_STEP = 8
FN2_K1_PER_STEP = 16
FN1_ROWS = SEQ_N1 * FN1_S2_PER_STEP


FN1_STEPS = SEQ_N2 // FN1_S2_PER_STEP
FN2_STEPS = SEQ_N1 // FN2_K1_PER_STEP


def _fnet_kernel(a_ref, b_ref, w1_ref, tc_ref, ts_ref, w2_ref, y_ref, pr_ref, pi_ref):
    j = pl.program_id(1)

    @pl.when(j < FN1_STEPS)
    def _stage1():
        flat = (N_FOURIER_GROUPS * FN1_ROWS, FOURIER_GROUP)
        a2, b2 = a_ref.reshape(*flat), b_ref.reshape(*flat)
        tcos = tc_ref[...]
        tsin = ts_ref[...]

        def rows(grp, jj):
            return pl.ds(grp * FN1_ROWS + jj, SEQ_N1, stride=FN1_S2_PER_STEP)

        def gather(ref2, jj):
            return jnp.concatenate([ref2[rows(grp, jj), :] for grp in range(N_FOURIER_GROUPS)],
                                   axis=1)

        for jj in range(FN1_S2_PER_STEP):
            x = jnp.concatenate([gather(a2, jj).astype(BF16), gather(b2, jj).astype(BF16)],
                                axis=0)
            t = _dot(w1_ref[...], x)
            tr = t[:SEQ_N1]
            ti = t[SEQ_N1:]
            c = tcos[:, jj:jj + 1]
            s = tsin[:, jj:jj + 1]
            pr = tr * c + ti * s
            pi = ti * c - tr * s
            for grp in range(N_FOURIER_GROUPS):
                cols = slice(grp * FOURIER_GROUP, (grp + 1) * FOURIER_GROUP)
                pr_ref[j, rows(grp, jj), :] = pr[:, cols]
                pi_ref[j, rows(grp, jj), :] = pi[:, cols]

    @pl.when(j >= FN1_STEPS)
    def _stage2():
        kb = j - FN1_STEPS

        def slab(ref, kk):
            def tile(grp, blk):
                row0 = pl.multiple_of(
                    grp * FN1_ROWS + (kb * FN2_K1_PER_STEP + kk) * FN1_S2_PER_STEP,
                    FN1_S2_PER_STEP)
                return ref[blk, pl.ds(row0, FN1_S2_PER_STEP), :]
            return jnp.concatenate(
                [jnp.concatenate([tile(grp, blk) for blk in range(FN1_STEPS)], axis=0)
                 for grp in range(N_FOURIER_GROUPS)], axis=1).astype(BF16)

        for kk in range(FN2_K1_PER_STEP):
            x = jnp.concatenate([slab(pr_ref, kk), slab(pi_ref, kk)], axis=0)
            y_ref[:, kk, :] = _dot(w2_ref[...], x)


def _fnet(a, b, w1, w2, tcos, tsin, batch, seq):
    shape5 = (N_FOURIER_GROUPS, batch, SEQ_N1, SEQ_N2, FOURIER_GROUP)
    a5 = a.reshape(shape5)
    b5 = b.reshape(shape5)
    last1 = FN1_STEPS - 1
    in_spec = pl.BlockSpec((N_FOURIER_GROUPS, None, SEQ_N1, FN1_S2_PER_STEP, FOURIER_GROUP),
                           lambda bb, j: (0, bb, 0, jnp.minimum(j, last1), 0))
    tw_spec = pl.BlockSpec((None, SEQ_N1, FN1_S2_PER_STEP),
                           lambda bb, j: (jnp.minimum(j, last1), 0, 0))
    out_spec = pl.BlockSpec((None, SEQ_N2, FN2_K1_PER_STEP, D_FOURIER),
                            lambda bb, j: (bb, 0, jnp.maximum(j - FN1_STEPS, 0), 0))
    scratch = pltpu.VMEM((FN1_STEPS, N_FOURIER_GROUPS * FN1_ROWS, FOURIER_GROUP), F32)
    blk_bytes = N_FOURIER_GROUPS * FN1_ROWS * FOURIER_GROUP * 4
    vmem = (2 * FN1_STEPS * blk_bytes
            + 4 * blk_bytes
            + 2 * SEQ_N2 * FN2_K1_PER_STEP * D_FOURIER * 4
            + (8 << 20))
    y = pl.pallas_call(
        _fnet_kernel,
        grid=(batch, FN1_STEPS + FN2_STEPS),
        in_specs=[in_spec, in_spec, _const_spec((2 * SEQ_N1, 2 * SEQ_N1)), tw_spec, tw_spec,
                  _const_spec((SEQ_N2, 2 * SEQ_N2))],
        out_specs=out_spec,
        out_shape=jax.ShapeDtypeStruct((batch, SEQ_N2, SEQ_N1, D_FOURIER), F32),
        scratch_shapes=[scratch, scratch],
        compiler_params=_cparams(vmem, 2),
        name="fnet",
    )(a5, b5, w1, tcos, tsin, w2)
    return y.reshape(batch * seq, D_FOURIER)


def _dft_constants():
    def cs(n):
        idx = np.arange(n)
        ang = 2.0 * np.pi * ((idx[:, None] * idx[None, :]) % n) / n
        return np.cos(ang), np.sin(ang)

    cc, sc = cs(FOURIER_GROUP)
    chan = np.concatenate([cc, sc], axis=1)
    c1, s1 = cs(SEQ_N1)
    w1 = np.block([[c1, -s1], [-s1, -c1]])
    c2, s2 = cs(SEQ_N2)
    norm = 1.0 / math.sqrt(SEQ_N1 * SEQ_N2 * FOURIER_GROUP)
    w2 = np.concatenate([c2, s2], axis=1) * norm
    k1 = np.arange(SEQ_N1)
    s2i = np.arange(SEQ_N2)
    ang = 2.0 * np.pi * (k1[:, None] * s2i[None, :]) / (SEQ_N1 * SEQ_N2)
    nblk = SEQ_N2 // FN1_S2_PER_STEP

    def blocked(t):
        return t.reshape(SEQ_N1, nblk, FN1_S2_PER_STEP).transpose(1, 0, 2)

    return (jnp.asarray(chan, F32).astype(BF16), jnp.asarray(w1, F32).astype(BF16),
            jnp.asarray(w2, F32).astype(BF16),
            jnp.asarray(blocked(np.cos(ang)), F32), jnp.asarray(blocked(np.sin(ang)), F32))


def kernel(x, ffn1_w_gate, ffn1_w_up, ffn1_w_down, ln1_g, ln1_b, w_in, rpb, g_attn, g_fourier,
           w_out, ln2_g, ln2_b, ffn2_w_gate, ffn2_w_up, ffn2_w_down, ln3_g, ln3_b):
    batch, seq, d = x.shape
    assert d == D_MODEL and seq == SEQ_N1 * SEQ_N2 and seq % (ATT_PAIRS_PER_STEP * PAIR_Q) == 0
    assert rpb.shape[0] == DEPTH
    chan, w1, w2, tcos, tsin = _dft_constants()

    def row(p, l):
        return p[l].reshape(1, -1).astype(F32)

    wg1, wu1, wd1, win, wout, wg2, wu2, wd2 = _cast_weights(
        [ffn1_w_gate, ffn1_w_up, ffn1_w_down, w_in, w_out, ffn2_w_gate, ffn2_w_up, ffn2_w_down])

    h = x.reshape(batch * seq, d).astype(F32)
    for l in range(DEPTH):
        h, q, k, v, a, b = _ffn_proj(l, h, wg1, wu1, wd1, row(ln1_g, l), row(ln1_b, l), win, chan)
        attn = _nattn(q, k, v, _toeplitz_tables(rpb[l]), batch, seq)
        four = _fnet(a, b, w1, w2, tcos, tsin, batch, seq)
        h = _mix_ffn(l, attn, four, h, row(g_attn, l), row(g_fourier, l), wout,
                     row(ln2_g, l), row(ln2_b, l), wg2, wu2, wd2, row(ln3_g, l), row(ln3_b, l))
    return h.reshape(batch, seq, d).astype(x.dtype)
```

```python
import functools
import math

import jax
import jax.numpy as jnp
import numpy as np
from jax import lax
from jax.experimental import pallas as pl
from jax.experimental.pallas import tpu as pltpu

D_MODEL = 1024
DEPTH = 2
GRID_W = 64
WIN_ROWS = 8
WIN_COLS = 16
N_ATTN_HEADS = 8
HEAD_DIM = 64
D_ATTN = N_ATTN_HEADS * HEAD_DIM
N_FOURIER_GROUPS = 4
FOURIER_GROUP = 128
D_FOURIER = N_FOURIER_GROUPS * FOURIER_GROUP
D_IN = 3 * D_ATTN + D_FOURIER
D_FF = 2816
ALPHA = (2.0 * DEPTH) ** 0.25
LN_EPS = 1e-5
RMS_EPS = 1e-6
NEG_INF = -1e30

V7X_LANES = 128
V7X_MXU_DIM = 256
V7X_VMEM_BYTES = 64 * 1024 * 1024

F32 = jnp.float32
BF16 = jnp.bfloat16


def _cparams(vmem_bytes, n_axes):
    return pltpu.CompilerParams(
        dimension_semantics=("arbitrary",) * n_axes,
        vmem_limit_bytes=min(int(vmem_bytes), V7X_VMEM_BYTES - (2 << 20)),
    )


def _const_spec(shape):
    nd = len(shape)
    return pl.BlockSpec(shape, lambda *_: (0,) * nd, pipeline_mode=pl.Buffered(1))


def _layer_norm(z, g, b):
    mu = jnp.mean(z, axis=-1, keepdims=True)
    zc = z - mu
    var = jnp.mean(zc * zc, axis=-1, keepdims=True)
    return zc * lax.rsqrt(var + LN_EPS) * g + b


def _rms_norm(z, g):
    return z * lax.rsqrt(jnp.mean(z * z, axis=-1, keepdims=True) + RMS_EPS) * g


def _dot(a, b):
    return jnp.dot(a, b, preferred_element_type=F32)


TOK_TM = 512
FFN_FC = V7X_MXU_DIM
Q_SCALE = HEAD_DIM ** -0.5 * math.log2(math.e)


TOK_SUB = 256
N_SUB = TOK_TM // TOK_SUB


def _sub_rows(s):
    return slice(s * TOK_SUB, (s + 1) * TOK_SUB)


def _swiglu_ln(xs, wg_ref, wu_ref, wd_ref, g_ref, b_ref, h_ref):
    xbs = [x.astype(BF16) for x in xs]
    for c in range(D_FF // FFN_FC):
        cols = slice(c * FFN_FC, (c + 1) * FFN_FC)
        for s, xb in enumerate(xbs):
            gate = _dot(xb, wg_ref[:, cols])
            up = _dot(xb, wu_ref[:, cols])
            h_ref[s, :, cols] = (gate * jax.nn.sigmoid(gate) * up).astype(BF16)
    outs = []
    for s, x in enumerate(xs):
        y = _dot(h_ref[s], wd_ref[...])
        outs.append(_layer_norm(ALPHA * x + 0.5 * y, g_ref[...], b_ref[...]))
    return outs


def _ffn_proj_kernel(x_ref, wg_ref, wu_ref, wd_ref, g_ref, b_ref, win_ref, cs_ref,
                     x1_ref, q_ref, k_ref, v_ref, a_ref, bb_ref, h_ref):
    x1s = _swiglu_ln([x_ref[_sub_rows(s), :] for s in range(N_SUB)],
                     wg_ref, wu_ref, wd_ref, g_ref, b_ref, h_ref)
    for s, x1 in enumerate(x1s):
        rows = _sub_rows(s)
        x1_ref[rows, :] = x1
        xb = x1.astype(BF16)
        q_ref[rows, :] = (_dot(xb, win_ref[:, 0:D_ATTN]) * Q_SCALE).astype(BF16)
        k_ref[rows, :] = _dot(xb, win_ref[:, D_ATTN:2 * D_ATTN]).astype(BF16)
        v_ref[rows, :] = _dot(xb, win_ref[:, 2 * D_ATTN:3 * D_ATTN]).astype(BF16)
        u = _dot(xb, win_ref[:, 3 * D_ATTN:]).astype(BF16)
        s1_per_sub = TOK_SUB // SEQ_N2
        for grp in range(N_FOURIER_GROUPS):
            cols = slice(grp * FOURIER_GROUP, (grp + 1) * FOURIER_GROUP)
            ab = _dot(u[:, cols], cs_ref[...])
            for i1 in range(s1_per_sub):
                for blk in range(FN1_STEPS):
                    r = i1 * SEQ_N2 + blk * FN1_S2_PER_STEP
                    slab = ab[r:r + FN1_S2_PER_STEP]
                    a_ref[grp, blk, s * s1_per_sub + i1] = slab[:, :FOURIER_GROUP]
                    bb_ref[grp, blk, s * s1_per_sub + i1] = slab[:, FOURIER_GROUP:]


def _mix_ffn_kernel(attn_ref, four_ref, x_ref, ga_ref, gf_ref, wo_ref, g2_ref, b2_ref,
                    wg_ref, wu_ref, wd_ref, g3_ref, b3_ref, o_ref, h_ref):
    x2s = []
    for s in range(N_SUB):
        rows = _sub_rows(s)
        ma = _rms_norm(attn_ref[rows, :], ga_ref[...]).astype(BF16)
        mf = _rms_norm(four_ref[rows, :], gf_ref[...]).astype(BF16)
        z = _dot(ma, wo_ref[0:D_ATTN, :]) + _dot(mf, wo_ref[D_ATTN:, :])
        x2s.append(_layer_norm(ALPHA * x_ref[rows, :] + z, g2_ref[...], b2_ref[...]))
    outs = _swiglu_ln(x2s, wg_ref, wu_ref, wd_ref, g3_ref, b3_ref, h_ref)
    for s, out in enumerate(outs):
        o_ref[_sub_rows(s), :] = out


def _tok_spec(width):
    return pl.BlockSpec((TOK_TM, width), lambda i: (i, 0))


def _layer_spec(layer, rows, cols):
    return pl.BlockSpec((None, rows, cols), lambda i: (layer, 0, 0), pipeline_mode=pl.Buffered(1))


def _ffn_weight_specs(layer):
    return [_layer_spec(layer, D_MODEL, D_FF), _layer_spec(layer, D_MODEL, D_FF),
            _layer_spec(layer, D_FF, D_MODEL), _const_spec((1, D_MODEL)), _const_spec((1, D_MODEL))]


_FFN_VMEM = 3 * D_MODEL * D_FF * 2 + TOK_TM * D_FF * 2 + 8 * TOK_TM * D_MODEL * 4

CAST_STEPS = 16


def _cast_kernel(*refs):
    n = len(refs) // 2
    for src, dst in zip(refs[:n], refs[n:]):
        dst[...] = src[...].astype(BF16)


def _cast_weights(weights):
    flat = [w.reshape(-1, w.shape[-1]) for w in weights]
    specs = [pl.BlockSpec((f.shape[0] // CAST_STEPS, f.shape[1]), lambda i: (i, 0)) for f in flat]
    vmem = 2 * sum(f.size // CAST_STEPS * 6 for f in flat) + (4 << 20)
    outs = pl.pallas_call(
        _cast_kernel,
        grid=(CAST_STEPS,),
        in_specs=specs,
        out_specs=specs,
        out_shape=[jax.ShapeDtypeStruct(f.shape, BF16) for f in flat],
        compiler_params=_cparams(vmem, 1),
        name="cast_weights",
    )(*flat)
    return [o.reshape(w.shape) for o, w in zip(outs, weights)]


def _ffn_proj(layer, x, wg, wu, wd, g, b, w_in, cs, batch):
    n = x.shape[0]
    half = jax.ShapeDtypeStruct((n, D_ATTN), BF16)
    s1_per_tile = TOK_TM // SEQ_N2
    tiles_per_batch = SEQ_N1 // s1_per_tile
    four = jax.ShapeDtypeStruct(
        (N_FOURIER_GROUPS, batch, FN1_STEPS, SEQ_N1, FN1_S2_PER_STEP, FOURIER_GROUP), F32)
    four_spec = pl.BlockSpec(
        (N_FOURIER_GROUPS, None, FN1_STEPS, s1_per_tile, FN1_S2_PER_STEP, FOURIER_GROUP),
        lambda i: (0, i // tiles_per_batch, 0, i % tiles_per_batch, 0, 0))
    vmem = (_FFN_VMEM + D_MODEL * D_IN * 2
            + 2 * TOK_TM * (2 * D_MODEL * 4 + 3 * D_ATTN * 2 + 2 * D_FOURIER * 4)
            + 2 * TOK_TM * D_IN * 4)
    return pl.pallas_call(
        _ffn_proj_kernel,
        grid=(n // TOK_TM,),
        in_specs=([_tok_spec(D_MODEL)] + _ffn_weight_specs(layer)
                  + [_layer_spec(layer, D_MODEL, D_IN),
                     _const_spec((FOURIER_GROUP, 2 * FOURIER_GROUP))]),
        out_specs=[_tok_spec(D_MODEL)] + [_tok_spec(D_ATTN)] * 3 + [four_spec] * 2,
        out_shape=[jax.ShapeDtypeStruct((n, D_MODEL), F32)] + [half] * 3 + [four] * 2,
        scratch_shapes=[pltpu.VMEM((N_SUB, TOK_SUB, D_FF), BF16)],
        compiler_params=_cparams(vmem, 1),
        name="ffn_proj",
    )(x, wg, wu, wd, g, b, w_in, cs)


def _mix_ffn(layer, attn, four, x, ga, gf, wo, g2, b2, wg, wu, wd, g3, b3):
    n = x.shape[0]
    vmem = (_FFN_VMEM + D_MODEL * D_MODEL * 2
            + 2 * TOK_TM * (2 * D_MODEL * 4 + D_ATTN * 4 + D_FOURIER * 4))
    return pl.pallas_call(
        _mix_ffn_kernel,
        grid=(n // TOK_TM,),
        in_specs=([_tok_spec(D_ATTN), _tok_spec(D_FOURIER), _tok_spec(D_MODEL),
                   _const_spec((1, D_ATTN)), _const_spec((1, D_FOURIER)),
                   _layer_spec(layer, D_MODEL, D_MODEL), _const_spec((1, D_MODEL)),
                   _const_spec((1, D_MODEL))]
                  + _ffn_weight_specs(layer)),
        out_specs=_tok_spec(D_MODEL),
        out_shape=jax.ShapeDtypeStruct((n, D_MODEL), F32),
        scratch_shapes=[pltpu.VMEM((N_SUB, TOK_SUB, D_FF), BF16)],
        compiler_params=_cparams(vmem, 1),
        name="mix_ffn",
    )(attn, four, x, ga, gf, wo, g2, b2, wg, wu, wd, g3, b3)


PAIR_Q = 2 * GRID_W
PAIR_KROWS = WIN_ROWS + 2
PAIR_K = PAIR_KROWS * GRID_W
ATT_PAIRS_PER_STEP = 16
ATT_PAIRS_PER_ITER = 4
N_TZ = 2 * WIN_ROWS
LOG2E = math.log2(math.e)
ATT_HALO = (WIN_ROWS // 2) * GRID_W
ATT_KV_WINDOW = ATT_PAIRS_PER_STEP * PAIR_Q + PAIR_K - PAIR_Q


def _kv_window_start(step, seq):
    tiles = jnp.clip(step * (ATT_PAIRS_PER_STEP * PAIR_Q // V7X_LANES) - ATT_HALO // V7X_LANES,
                     0, (seq - ATT_KV_WINDOW) // V7X_LANES)
    return tiles * V7X_LANES


def _toeplitz_tables(rpb_l):
    cols = np.arange(GRID_W)
    c0 = np.clip(cols - WIN_COLS // 2, 0, GRID_W - WIN_COLS)
    in_win = (cols[None, :] >= c0[:, None]) & (cols[None, :] < c0[:, None] + WIN_COLS)
    dc_idx = np.clip(cols[None, :] - cols[:, None], -(WIN_COLS - 1), WIN_COLS - 1) + (WIN_COLS - 1)
    onehot = (dc_idx[None] == np.arange(2 * WIN_COLS - 1)[:, None, None]).astype(np.float32)
    tz = jnp.einsum("hdj,jqk->hdqk", rpb_l.astype(F32) * LOG2E, jnp.asarray(onehot),
                    precision=lax.Precision.HIGHEST)
    tz = jnp.where(in_win, tz, NEG_INF)
    tzp = jnp.pad(tz, ((0, 0), (1, 1), (0, 0), (0, 0)), constant_values=NEG_INF)
    left, right = tzp[:, :-1], tzp[:, 1:]
    masked = jnp.full_like(left, NEG_INF)
    return jnp.concatenate([jnp.concatenate([left, right], axis=-1),
                            jnp.concatenate([left, masked], axis=-1),
                            jnp.concatenate([masked, right], axis=-1)], axis=1)


def _nattn_kernel(q_ref, k_ref, v_ref, tz_ref, o_ref, s_ref, *, n_rows):
    step = pl.program_id(1)
    kv_start = _kv_window_start(step, n_rows * GRID_W)
    lane = lax.broadcasted_iota(jnp.int32, (PAIR_Q, V7X_LANES), 1)
    first_head = lane < HEAD_DIM
    ones_block = jnp.ones((PAIR_K, V7X_LANES), BF16)

    def pair_geometry(pl_idx):
        p = step * ATT_PAIRS_PER_STEP + pl_idx
        w = jnp.clip(2 * p - WIN_ROWS // 2, 0, n_rows - PAIR_KROWS)

        def tz_index(a, t):
            r = 2 * p + a
            r0 = jnp.clip(r - WIN_ROWS // 2, 0, n_rows - WIN_ROWS)
            kr = w + 2 * t
            ok0 = (kr >= r0) & (kr < r0 + WIN_ROWS)
            ok1 = (kr + 1 >= r0) & (kr + 1 < r0 + WIN_ROWS)
            i = jnp.clip(kr - r + WIN_ROWS, 0, N_TZ - 1)
            return jnp.where(ok0 & ok1, i,
                             jnp.where(ok0, N_TZ + i, jnp.where(ok1, 2 * N_TZ + i, N_TZ)))

        return dict(
            qrows=pl.ds(pl.multiple_of(pl_idx * PAIR_Q, PAIR_Q), PAIR_Q),
            krows=pl.ds(pl.multiple_of(w * GRID_W - kv_start, V7X_LANES), PAIR_K),
            tz_idx=[[tz_index(a, t) for t in range(PAIR_KROWS // 2)] for a in (0, 1)])

    def scores(geo, hp):
        lanes = slice(hp * V7X_LANES, (hp + 1) * V7X_LANES)
        qp = q_ref[geo["qrows"], lanes]
        kp = k_ref[geo["krows"], lanes]
        zero = jnp.zeros_like(qp)
        qq = jnp.concatenate([jnp.where(first_head, qp, zero),
                              jnp.where(first_head, zero, qp)], axis=0)
        return lax.dot_general(qq, kp, (((1,), (1,)), ((), ())),
                               preferred_element_type=F32)

    def finish(geo, hp, s):
        lanes = slice(hp * V7X_LANES, (hp + 1) * V7X_LANES)
        probs = []
        for e in (0, 1):
            h = 2 * hp + e
            bias = jnp.concatenate(
                [jnp.concatenate([tz_ref[h, geo["tz_idx"][a][t]] for t in range(PAIR_KROWS // 2)],
                                 axis=1) for a in (0, 1)], axis=0)
            se = s[e * PAIR_Q:(e + 1) * PAIR_Q] + bias
            m = jnp.max(se, axis=-1, keepdims=True)
            probs.append(jnp.exp2(se - m).astype(BF16))
        v_ext = jnp.concatenate([v_ref[geo["krows"], lanes], ones_block], axis=1)
        o = _dot(jnp.concatenate(probs, axis=0), v_ext)
        o = o[:, :V7X_LANES] / o[:, V7X_LANES:]
        o_ref[geo["qrows"], lanes] = jnp.where(first_head, o[:PAIR_Q], o[PAIR_Q:])

    def body(it, carry):
        geos = [pair_geometry(it * ATT_PAIRS_PER_ITER + j) for j in range(ATT_PAIRS_PER_ITER)]
        units = [(geos[j], hp) for j in range(ATT_PAIRS_PER_ITER)
                 for hp in range(N_ATTN_HEADS // 2)]
        s_cur = s_ref[...]
        for u in range(1, len(units)):
            s_next = scores(*units[u])
            finish(*units[u - 1], s_cur)
            s_cur = s_next
        nxt = jnp.minimum((it + 1) * ATT_PAIRS_PER_ITER, ATT_PAIRS_PER_STEP - 1)
        s_ref[...] = scores(pair_geometry(nxt), 0)
        finish(*units[-1], s_cur)
        return carry

    s_ref[...] = scores(pair_geometry(0), 0)
    lax.fori_loop(0, ATT_PAIRS_PER_STEP // ATT_PAIRS_PER_ITER, body, 0)


def _nattn(q, k, v, tz2, batch, seq):
    n_rows = seq // GRID_W
    tq = ATT_PAIRS_PER_STEP * PAIR_Q
    steps = seq // tq
    k3 = k.reshape(batch, seq, D_ATTN)
    v3 = v.reshape(batch, seq, D_ATTN)
    vmem = (2 * 2 * ATT_KV_WINDOW * D_ATTN * 2
            + tz2.size * 4
            + 2 * tq * D_ATTN * (2 + 4)
            + (8 << 20))
    kv_spec = pl.BlockSpec(
        (None, pl.Element(ATT_KV_WINDOW), pl.Element(D_ATTN)),
        lambda b, i: (b, _kv_window_start(i, seq), 0))
    return pl.pallas_call(
        functools.partial(_nattn_kernel, n_rows=n_rows),
        grid=(batch, steps),
        in_specs=[
            pl.BlockSpec((tq, D_ATTN), lambda b, i: (b * steps + i, 0)),
            kv_spec,
            kv_spec,
            _const_spec(tz2.shape),
        ],
        out_specs=pl.BlockSpec((tq, D_ATTN), lambda b, i: (b * steps + i, 0)),
        out_shape=jax.ShapeDtypeStruct((batch * seq, D_ATTN), F32),
        scratch_shapes=[pltpu.VMEM((2 * PAIR_Q, PAIR_K), F32)],
        compiler_params=_cparams(vmem, 2),
        name="nattn",
    )(q, k3, v3, tz2)


SEQ_N1 = 128
SEQ_N2 = 64
FN1_S2_PER_STEP = 8
FN2_K1_PER_STEP = 16
FN1_ROWS = SEQ_N1 * FN1_S2_PER_STEP


FN1_STEPS = SEQ_N2 // FN1_S2_PER_STEP
FN2_STEPS = SEQ_N1 // FN2_K1_PER_STEP


def _fnet_kernel(a_ref, b_ref, w1_ref, tc_ref, ts_ref, w2_ref, y_ref, pr_ref, pi_ref):
    j = pl.program_id(1)

    @pl.when(j < FN1_STEPS)
    def _stage1():
        flat = (N_FOURIER_GROUPS * FN1_ROWS, FOURIER_GROUP)
        a2, b2 = a_ref.reshape(*flat), b_ref.reshape(*flat)
        tcos = tc_ref[...]
        tsin = ts_ref[...]

        def rows(grp, jj):
            return pl.ds(grp * FN1_ROWS + jj, SEQ_N1, stride=FN1_S2_PER_STEP)

        def gather(ref2, jj):
            return jnp.concatenate([ref2[rows(grp, jj), :] for grp in range(N_FOURIER_GROUPS)],
                                   axis=1)

        for jj in range(FN1_S2_PER_STEP):
            x = jnp.concatenate([gather(a2, jj).astype(BF16), gather(b2, jj).astype(BF16)],
                                axis=0)
            t = _dot(w1_ref[...], x)
            tr = t[:SEQ_N1]
            ti = t[SEQ_N1:]
            c = tcos[:, jj:jj + 1]
            s = tsin[:, jj:jj + 1]
            pr = tr * c + ti * s
            pi = ti * c - tr * s
            for grp in range(N_FOURIER_GROUPS):
                cols = slice(grp * FOURIER_GROUP, (grp + 1) * FOURIER_GROUP)
                pr_ref[j, rows(grp, jj), :] = pr[:, cols]
                pi_ref[j, rows(grp, jj), :] = pi[:, cols]

    @pl.when(j >= FN1_STEPS)
    def _stage2():
        kb = j - FN1_STEPS

        def slab(ref, kk):
            def tile(grp, blk):
                row0 = pl.multiple_of(
                    grp * FN1_ROWS + (kb * FN2_K1_PER_STEP + kk) * FN1_S2_PER_STEP,
                    FN1_S2_PER_STEP)
                return ref[blk, pl.ds(row0, FN1_S2_PER_STEP), :]
            return jnp.concatenate(
                [jnp.concatenate([tile(grp, blk) for blk in range(FN1_STEPS)], axis=0)
                 for grp in range(N_FOURIER_GROUPS)], axis=1).astype(BF16)

        for kk in range(FN2_K1_PER_STEP):
            x = jnp.concatenate([slab(pr_ref, kk), slab(pi_ref, kk)], axis=0)
            y_ref[:, kk, :] = _dot(w2_ref[...], x)


def _fnet(a, b, w1, w2, tcos, tsin, batch, seq):
    last1 = FN1_STEPS - 1
    in_spec = pl.BlockSpec(
        (N_FOURIER_GROUPS, None, None, SEQ_N1, FN1_S2_PER_STEP, FOURIER_GROUP),
        lambda bb, j: (0, bb, jnp.minimum(j, last1), 0, 0, 0))
    tw_spec = pl.BlockSpec((None, SEQ_N1, FN1_S2_PER_STEP),
                           lambda bb, j: (jnp.minimum(j, last1), 0, 0))
    out_spec = pl.BlockSpec((None, SEQ_N2, FN2_K1_PER_STEP, D_FOURIER),
                            lambda bb, j: (bb, 0, jnp.maximum(j - FN1_STEPS, 0), 0))
    scratch = pltpu.VMEM((FN1_STEPS, N_FOURIER_GROUPS * FN1_ROWS, FOURIER_GROUP), F32)
    blk_bytes = N_FOURIER_GROUPS * FN1_ROWS * FOURIER_GROUP * 4
    vmem = (2 * FN1_STEPS * blk_bytes
            + 4 * blk_bytes
            + 2 * SEQ_N2 * FN2_K1_PER_STEP * D_FOURIER * 4
            + (8 << 20))
    y = pl.pallas_call(
        _fnet_kernel,
        grid=(batch, FN1_STEPS + FN2_STEPS),
        in_specs=[in_spec, in_spec, _const_spec((2 * SEQ_N1, 2 * SEQ_N1)), tw_spec, tw_spec,
                  _const_spec((SEQ_N2, 2 * SEQ_N2))],
        out_specs=out_spec,
        out_shape=jax.ShapeDtypeStruct((batch, SEQ_N2, SEQ_N1, D_FOURIER), F32),
        scratch_shapes=[scratch, scratch],
        compiler_params=_cparams(vmem, 2),
        name="fnet",
    )(a, b, w1, tcos, tsin, w2)
    return y.reshape(batch * seq, D_FOURIER)


def _dft_constants():
    def cs(n):
        idx = np.arange(n)
        ang = 2.0 * np.pi * ((idx[:, None] * idx[None, :]) % n) / n
        return np.cos(ang), np.sin(ang)

    cc, sc = cs(FOURIER_GROUP)
    chan = np.concatenate([cc, sc], axis=1)
    c1, s1 = cs(SEQ_N1)
    w1 = np.block([[c1, -s1], [-s1, -c1]])
    c2, s2 = cs(SEQ_N2)
    norm = 1.0 / math.sqrt(SEQ_N1 * SEQ_N2 * FOURIER_GROUP)
    w2 = np.concatenate([c2, s2], axis=1) * norm
    k1 = np.arange(SEQ_N1)
    s2i = np.arange(SEQ_N2)
    ang = 2.0 * np.pi * (k1[:, None] * s2i[None, :]) / (SEQ_N1 * SEQ_N2)
    nblk = SEQ_N2 // FN1_S2_PER_STEP

    def blocked(t):
        return t.reshape(SEQ_N1, nblk, FN1_S2_PER_STEP).transpose(1, 0, 2)

    return (jnp.asarray(chan, F32).astype(BF16), jnp.asarray(w1, F32).astype(BF16),
            jnp.asarray(w2, F32).astype(BF16),
            jnp.asarray(blocked(np.cos(ang)), F32), jnp.asarray(blocked(np.sin(ang)), F32))


def kernel(x, ffn1_w_gate, ffn1_w_up, ffn1_w_down, ln1_g, ln1_b, w_in, rpb, g_attn, g_fourier,
           w_out, ln2_g, ln2_b, ffn2_w_gate, ffn2_w_up, ffn2_w_down, ln3_g, ln3_b):
    batch, seq, d = x.shape
    assert d == D_MODEL and seq == SEQ_N1 * SEQ_N2 and seq % (ATT_PAIRS_PER_STEP * PAIR_Q) == 0
    assert rpb.shape[0] == DEPTH
    chan, w1, w2, tcos, tsin = _dft_constants()

    def row(p, l):
        return p[l].reshape(1, -1).astype(F32)

    wg1, wu1, wd1, win, wout, wg2, wu2, wd2 = _cast_weights(
        [ffn1_w_gate, ffn1_w_up, ffn1_w_down, w_in, w_out, ffn2_w_gate, ffn2_w_up, ffn2_w_down])

    h = x.reshape(batch * seq, d).astype(F32)
    for l in range(DEPTH):
        h, q, k, v, a, b = _ffn_proj(l, h, wg1, wu1, wd1, row(ln1_g, l), row(ln1_b, l), win, chan,
                                     batch)
        attn = _nattn(q, k, v, _toeplitz_tables(rpb[l]), batch, seq)
        four = _fnet(a, b, w1, w2, tcos, tsin, batch, seq)
        h = _mix_ffn(l, attn, four, h, row(g_attn, l), row(g_fourier, l), wout,
                     row(ln2_g, l), row(ln2_b, l), wg2, wu2, wd2, row(ln3_g, l), row(ln3_b, l))
    return h.reshape(batch, seq, d).astype(x.dtype)
```

```python
import functools
import math

import jax
import jax.numpy as jnp
import numpy as np
from jax import lax
from jax.experimental import pallas as pl
from jax.experimental.pallas import tpu as pltpu

D_MODEL = 1024
DEPTH = 2
GRID_W = 64
WIN_ROWS = 8
WIN_COLS = 16
N_ATTN_HEADS = 8
HEAD_DIM = 64
D_ATTN = N_ATTN_HEADS * HEAD_DIM
N_FOURIER_GROUPS = 4
FOURIER_GROUP = 128
D_FOURIER = N_FOURIER_GROUPS * FOURIER_GROUP
D_IN = 3 * D_ATTN + D_FOURIER
D_FF = 2816
ALPHA = (2.0 * DEPTH) ** 0.25
LN_EPS = 1e-5
RMS_EPS = 1e-6
NEG_INF = -1e30

V7X_LANES = 128
V7X_MXU_DIM = 256
V7X_VMEM_BYTES = 64 * 1024 * 1024

F32 = jnp.float32
BF16 = jnp.bfloat16


def _cparams(vmem_bytes, n_axes):
    return pltpu.CompilerParams(
        dimension_semantics=("arbitrary",) * n_axes,
        vmem_limit_bytes=min(int(vmem_bytes), V7X_VMEM_BYTES - (2 << 20)),
    )


def _const_spec(shape):
    nd = len(shape)
    return pl.BlockSpec(shape, lambda *_: (0,) * nd, pipeline_mode=pl.Buffered(1))


def _layer_norm(z, g, b):
    mu = jnp.mean(z, axis=-1, keepdims=True)
    zc = z - mu
    var = jnp.mean(zc * zc, axis=-1, keepdims=True)
    return zc * lax.rsqrt(var + LN_EPS) * g + b


def _rms_norm(z, g):
    return z * lax.rsqrt(jnp.mean(z * z, axis=-1, keepdims=True) + RMS_EPS) * g


def _dot(a, b):
    return jnp.dot(a, b, preferred_element_type=F32)


TOK_TM = 512
MIX_TM = 1024
FFN_FC = V7X_MXU_DIM
Q_SCALE = HEAD_DIM ** -0.5 * math.log2(math.e)


TOK_SUB = 256
N_SUB = TOK_TM // TOK_SUB


def _sub_rows(s):
    return slice(s * TOK_SUB, (s + 1) * TOK_SUB)


def _swiglu_ln(xs, wg_ref, wu_ref, wd_ref, g_ref, b_ref, h_ref):
    xbs = [x.astype(BF16) for x in xs]
    for c in range(D_FF // FFN_FC):
        cols = slice(c * FFN_FC, (c + 1) * FFN_FC)
        for s, xb in enumerate(xbs):
            gate = _dot(xb, wg_ref[:, cols])
            up = _dot(xb, wu_ref[:, cols])
            h_ref[s, :, cols] = (gate * jax.nn.sigmoid(gate) * up).astype(BF16)
    outs = []
    for s, x in enumerate(xs):
        y = _dot(h_ref[s], wd_ref[...])
        outs.append(_layer_norm(ALPHA * x + 0.5 * y, g_ref[...], b_ref[...]))
    return outs


def _ffn_proj_kernel(x_ref, wg_ref, wu_ref, wd_ref, g_ref, b_ref, win_ref, cs_ref,
                     x1_ref, q_ref, k_ref, v_ref, a_ref, bb_ref, h_ref):
    x1s = _swiglu_ln([x_ref[_sub_rows(s), :] for s in range(N_SUB)],
                     wg_ref, wu_ref, wd_ref, g_ref, b_ref, h_ref)
    for s, x1 in enumerate(x1s):
        rows = _sub_rows(s)
        x1_ref[rows, :] = x1
        xb = x1.astype(BF16)
        q_ref[rows, :] = (_dot(xb, win_ref[:, 0:D_ATTN]) * Q_SCALE).astype(BF16)
        k_ref[rows, :] = _dot(xb, win_ref[:, D_ATTN:2 * D_ATTN]).astype(BF16)
        v_ref[rows, :] = _dot(xb, win_ref[:, 2 * D_ATTN:3 * D_ATTN]).astype(BF16)
        u = _dot(xb, win_ref[:, 3 * D_ATTN:]).astype(BF16)
        for grp in range(N_FOURIER_GROUPS):
            cols = slice(grp * FOURIER_GROUP, (grp + 1) * FOURIER_GROUP)
            ab = _dot(u[:, cols], cs_ref[...])
            a_ref[grp, rows, :] = ab[:, :FOURIER_GROUP]
            bb_ref[grp, rows, :] = ab[:, FOURIER_GROUP:]


def _mix_ffn_kernel(attn_ref, four_ref, x_ref, ga_ref, gf_ref, wo_ref, g2_ref, b2_ref,
                    wg_ref, wu_ref, wd_ref, g3_ref, b3_ref, o_ref, h_ref):
    x2s = []
    for s in range(MIX_TM // TOK_SUB):
        rows = _sub_rows(s)
        ma = _rms_norm(attn_ref[rows, :], ga_ref[...]).astype(BF16)
        mf = _rms_norm(four_ref[rows, :], gf_ref[...]).astype(BF16)
        z = _dot(ma, wo_ref[0:D_ATTN, :]) + _dot(mf, wo_ref[D_ATTN:, :])
        x2s.append(_layer_norm(ALPHA * x_ref[rows, :] + z, g2_ref[...], b2_ref[...]))
    outs = _swiglu_ln(x2s, wg_ref, wu_ref, wd_ref, g3_ref, b3_ref, h_ref)
    for s, out in enumerate(outs):
        o_ref[_sub_rows(s), :] = out


def _tok_spec(width, tm=TOK_TM):
    return pl.BlockSpec((tm, width), lambda i: (i, 0))


def _layer_spec(layer, rows, cols):
    return pl.BlockSpec((None, rows, cols), lambda i: (layer, 0, 0), pipeline_mode=pl.Buffered(1))


def _ffn_weight_specs(layer):
    return [_layer_spec(layer, D_MODEL, D_FF), _layer_spec(layer, D_MODEL, D_FF),
            _layer_spec(layer, D_FF, D_MODEL), _const_spec((1, D_MODEL)), _const_spec((1, D_MODEL))]


_FFN_VMEM = 3 * D_MODEL * D_FF * 2 + TOK_TM * D_FF * 2 + 8 * TOK_TM * D_MODEL * 4

CAST_STEPS = 16


def _cast_kernel(*refs):
    n = len(refs) // 2
    for src, dst in zip(refs[:n], refs[n:]):
        dst[...] = src[...].astype(BF16)


def _cast_weights(weights):
    flat = [w.reshape(-1, w.shape[-1]) for w in weights]
    specs = [pl.BlockSpec((f.shape[0] // CAST_STEPS, f.shape[1]), lambda i: (i, 0)) for f in flat]
    vmem = 2 * sum(f.size // CAST_STEPS * 6 for f in flat) + (4 << 20)
    outs = pl.pallas_call(
        _cast_kernel,
        grid=(CAST_STEPS,),
        in_specs=specs,
        out_specs=specs,
        out_shape=[jax.ShapeDtypeStruct(f.shape, BF16) for f in flat],
        compiler_params=_cparams(vmem, 1),
        name="cast_weights",
    )(*flat)
    return [o.reshape(w.shape) for o, w in zip(outs, weights)]


def _ffn_proj(layer, x, wg, wu, wd, g, b, w_in, cs):
    n = x.shape[0]
    half = jax.ShapeDtypeStruct((n, D_ATTN), BF16)
    four = jax.ShapeDtypeStruct((N_FOURIER_GROUPS, n, FOURIER_GROUP), F32)
    four_spec = pl.BlockSpec((N_FOURIER_GROUPS, TOK_TM, FOURIER_GROUP), lambda i: (0, i, 0))
    vmem = (_FFN_VMEM + D_MODEL * D_IN * 2
            + 2 * TOK_TM * (2 * D_MODEL * 4 + 3 * D_ATTN * 2 + 2 * D_FOURIER * 4)
            + 2 * TOK_TM * D_IN * 4)
    return pl.pallas_call(
        _ffn_proj_kernel,
        grid=(n // TOK_TM,),
        in_specs=([_tok_spec(D_MODEL)] + _ffn_weight_specs(layer)
                  + [_layer_spec(layer, D_MODEL, D_IN),
                     _const_spec((FOURIER_GROUP, 2 * FOURIER_GROUP))]),
        out_specs=[_tok_spec(D_MODEL)] + [_tok_spec(D_ATTN)] * 3 + [four_spec] * 2,
        out_shape=[jax.ShapeDtypeStruct((n, D_MODEL), F32)] + [half] * 3 + [four] * 2,
        scratch_shapes=[pltpu.VMEM((N_SUB, TOK_SUB, D_FF), BF16)],
        compiler_params=_cparams(vmem, 1),
        name="ffn_proj",
    )(x, wg, wu, wd, g, b, w_in, cs)


def _mix_ffn(layer, attn, four, x, ga, gf, wo, g2, b2, wg, wu, wd, g3, b3):
    n = x.shape[0]
    tm = MIX_TM
    vmem = (3 * D_MODEL * D_FF * 2 + D_MODEL * D_MODEL * 2 + tm * D_FF * 2
            + 2 * tm * (2 * D_MODEL * 4 + D_ATTN * 4 + D_FOURIER * 4)
            + 8 * TOK_TM * D_MODEL * 4)
    return pl.pallas_call(
        _mix_ffn_kernel,
        grid=(n // tm,),
        in_specs=([_tok_spec(D_ATTN, tm), _tok_spec(D_FOURIER, tm), _tok_spec(D_MODEL, tm),
                   _const_spec((1, D_ATTN)), _const_spec((1, D_FOURIER)),
                   _layer_spec(layer, D_MODEL, D_MODEL), _const_spec((1, D_MODEL)),
                   _const_spec((1, D_MODEL))]
                  + _ffn_weight_specs(layer)),
        out_specs=_tok_spec(D_MODEL, tm),
        out_shape=jax.ShapeDtypeStruct((n, D_MODEL), F32),
        scratch_shapes=[pltpu.VMEM((tm // TOK_SUB, TOK_SUB, D_FF), BF16)],
        compiler_params=_cparams(vmem, 1),
        name="mix_ffn",
    )(attn, four, x, ga, gf, wo, g2, b2, wg, wu, wd, g3, b3)


PAIR_Q = 2 * GRID_W
PAIR_KROWS = WIN_ROWS + 2
PAIR_K = PAIR_KROWS * GRID_W
ATT_PAIRS_PER_STEP = 16
ATT_PAIRS_PER_ITER = 4
N_TZ = 2 * WIN_ROWS
LOG2E = math.log2(math.e)
ATT_HALO = (WIN_ROWS // 2) * GRID_W
ATT_KV_WINDOW = ATT_PAIRS_PER_STEP * PAIR_Q + PAIR_K - PAIR_Q


def _kv_window_start(step, seq):
    tiles = jnp.clip(step * (ATT_PAIRS_PER_STEP * PAIR_Q // V7X_LANES) - ATT_HALO // V7X_LANES,
                     0, (seq - ATT_KV_WINDOW) // V7X_LANES)
    return tiles * V7X_LANES


def _toeplitz_tables(rpb_l):
    cols = np.arange(GRID_W)
    c0 = np.clip(cols - WIN_COLS // 2, 0, GRID_W - WIN_COLS)
    in_win = (cols[None, :] >= c0[:, None]) & (cols[None, :] < c0[:, None] + WIN_COLS)
    dc_idx = np.clip(cols[None, :] - cols[:, None], -(WIN_COLS - 1), WIN_COLS - 1) + (WIN_COLS - 1)
    onehot = (dc_idx[None] == np.arange(2 * WIN_COLS - 1)[:, None, None]).astype(np.float32)
    tz = jnp.einsum("hdj,jqk->hdqk", rpb_l.astype(F32) * LOG2E, jnp.asarray(onehot),
                    precision=lax.Precision.HIGHEST)
    tz = jnp.where(in_win, tz, NEG_INF)
    tzp = jnp.pad(tz, ((0, 0), (1, 1), (0, 0), (0, 0)), constant_values=NEG_INF)
    left, right = tzp[:, :-1], tzp[:, 1:]
    masked = jnp.full_like(left, NEG_INF)
    return jnp.concatenate([jnp.concatenate([left, right], axis=-1),
                            jnp.concatenate([left, masked], axis=-1),
                            jnp.concatenate([masked, right], axis=-1)], axis=1)


def _nattn_kernel(q_ref, k_ref, v_ref, tz_ref, o_ref, s_ref, *, n_rows):
    step = pl.program_id(1)
    kv_start = _kv_window_start(step, n_rows * GRID_W)
    lane = lax.broadcasted_iota(jnp.int32, (PAIR_Q, V7X_LANES), 1)
    first_head = lane < HEAD_DIM
    ones_block = jnp.ones((PAIR_K, V7X_LANES), BF16)

    def pair_geometry(pl_idx):
        p = step * ATT_PAIRS_PER_STEP + pl_idx
        w = jnp.clip(2 * p - WIN_ROWS // 2, 0, n_rows - PAIR_KROWS)

        def tz_index(a, t):
            r = 2 * p + a
            r0 = jnp.clip(r - WIN_ROWS // 2, 0, n_rows - WIN_ROWS)
            kr = w + 2 * t
            ok0 = (kr >= r0) & (kr < r0 + WIN_ROWS)
            ok1 = (kr + 1 >= r0) & (kr + 1 < r0 + WIN_ROWS)
            i = jnp.clip(kr - r + WIN_ROWS, 0, N_TZ - 1)
            return jnp.where(ok0 & ok1, i,
                             jnp.where(ok0, N_TZ + i, jnp.where(ok1, 2 * N_TZ + i, N_TZ)))

        return dict(
            qrows=pl.ds(pl.multiple_of(pl_idx * PAIR_Q, PAIR_Q), PAIR_Q),
            krows=pl.ds(pl.multiple_of(w * GRID_W - kv_start, V7X_LANES), PAIR_K),
            tz_idx=[[tz_index(a, t) for t in range(PAIR_KROWS // 2)] for a in (0, 1)])

    def scores(geo, hp):
        lanes = slice(hp * V7X_LANES, (hp + 1) * V7X_LANES)
        qp = q_ref[geo["qrows"], lanes]
        kp = k_ref[geo["krows"], lanes]
        zero = jnp.zeros_like(qp)
        qq = jnp.concatenate([jnp.where(first_head, qp, zero),
                              jnp.where(first_head, zero, qp)], axis=0)
        return lax.dot_general(qq, kp, (((1,), (1,)), ((), ())),
                               preferred_element_type=F32)

    def finish(geo, hp, s):
        lanes = slice(hp * V7X_LANES, (hp + 1) * V7X_LANES)
        probs = []
        for e in (0, 1):
            h = 2 * hp + e
            bias = jnp.concatenate(
                [jnp.concatenate([tz_ref[h, geo["tz_idx"][a][t]] for t in range(PAIR_KROWS // 2)],
                                 axis=1) for a in (0, 1)], axis=0)
            se = s[e * PAIR_Q:(e + 1) * PAIR_Q] + bias
            m = jnp.max(se, axis=-1, keepdims=True)
            probs.append(jnp.exp2(se - m).astype(BF16))
        v_ext = jnp.concatenate([v_ref[geo["krows"], lanes], ones_block], axis=1)
        o = _dot(jnp.concatenate(probs, axis=0), v_ext)
        o = o[:, :V7X_LANES] / o[:, V7X_LANES:]
        o_ref[geo["qrows"], lanes] = jnp.where(first_head, o[:PAIR_Q], o[PAIR_Q:])

    def body(it, carry):
        geos = [pair_geometry(it * ATT_PAIRS_PER_ITER + j) for j in range(ATT_PAIRS_PER_ITER)]
        units = [(geos[j], hp) for j in range(ATT_PAIRS_PER_ITER)
                 for hp in range(N_ATTN_HEADS // 2)]
        s_cur = s_ref[...]
        for u in range(1, len(units)):
            s_next = scores(*units[u])
            finish(*units[u - 1], s_cur)
            s_cur = s_next
        nxt = jnp.minimum((it + 1) * ATT_PAIRS_PER_ITER, ATT_PAIRS_PER_STEP - 1)
        s_ref[...] = scores(pair_geometry(nxt), 0)
        finish(*units[-1], s_cur)
        return carry

    s_ref[...] = scores(pair_geometry(0), 0)
    lax.fori_loop(0, ATT_PAIRS_PER_STEP // ATT_PAIRS_PER_ITER, body, 0)


def _nattn(q, k, v, tz2, batch, seq):
    n_rows = seq // GRID_W
    tq = ATT_PAIRS_PER_STEP * PAIR_Q
    steps = seq // tq
    k3 = k.reshape(batch, seq, D_ATTN)
    v3 = v.reshape(batch, seq, D_ATTN)
    vmem = (2 * 2 * ATT_KV_WINDOW * D_ATTN * 2
            + tz2.size * 4
            + 2 * tq * D_ATTN * (2 + 4)
            + (8 << 20))
    kv_spec = pl.BlockSpec(
        (None, pl.Element(ATT_KV_WINDOW), pl.Element(D_ATTN)),
        lambda b, i: (b, _kv_window_start(i, seq), 0))
    return pl.pallas_call(
        functools.partial(_nattn_kernel, n_rows=n_rows),
        grid=(batch, steps),
        in_specs=[
            pl.BlockSpec((tq, D_ATTN), lambda b, i: (b * steps + i, 0)),
            kv_spec,
            kv_spec,
            _const_spec(tz2.shape),
        ],
        out_specs=pl.BlockSpec((tq, D_ATTN), lambda b, i: (b * steps + i, 0)),
        out_shape=jax.ShapeDtypeStruct((batch * seq, D_ATTN), F32),
        scratch_shapes=[pltpu.VMEM((2 * PAIR_Q, PAIR_K), F32)],
        compiler_params=_cparams(vmem, 2),
        name="nattn",
    )(q, k3, v3, tz2)


SEQ_N1 = 128
SEQ_N2 = 64
FN1_S2_PER_STEP = 8
FN2_K1_PER_STEP = 16
FN1_ROWS = SEQ_N1 * FN1_S2_PER_STEP


FN1_STEPS = SEQ_N2 // FN1_S2_PER_STEP
FN2_STEPS = SEQ_N1 // FN2_K1_PER_STEP


def _fnet_kernel(a_ref, b_ref, w1_ref, tc_ref, ts_ref, w2_ref, y_ref, pr_ref, pi_ref):
    j = pl.program_id(1)

    @pl.when(j < FN1_STEPS)
    def _stage1():
        flat = (N_FOURIER_GROUPS * FN1_ROWS, FOURIER_GROUP)
        a2, b2 = a_ref.reshape(*flat), b_ref.reshape(*flat)
        tcos = tc_ref[...]
        tsin = ts_ref[...]

        def rows(grp, jj):
            return pl.ds(grp * FN1_ROWS + jj, SEQ_N1, stride=FN1_S2_PER_STEP)

        def gather(ref2, jj):
            return jnp.concatenate([ref2[rows(grp, jj), :] for grp in range(N_FOURIER_GROUPS)],
                                   axis=1)

        for jj in range(FN1_S2_PER_STEP):
            x = jnp.concatenate([gather(a2, jj).astype(BF16), gather(b2, jj).astype(BF16)],
                                axis=0)
            t = _dot(w1_ref[...], x)
            tr = t[:SEQ_N1]
            ti = t[SEQ_N1:]
            c = tcos[:, jj:jj + 1]
            s = tsin[:, jj:jj + 1]
            pr = tr * c + ti * s
            pi = ti * c - tr * s
            for grp in range(N_FOURIER_GROUPS):
                cols = slice(grp * FOURIER_GROUP, (grp + 1) * FOURIER_GROUP)
                pr_ref[j, rows(grp, jj), :] = pr[:, cols]
                pi_ref[j, rows(grp, jj), :] = pi[:, cols]

    @pl.when(j >= FN1_STEPS)
    def _stage2():
        kb = j - FN1_STEPS

        def slab(ref, kk):
            def tile(grp, blk):
                row0 = pl.multiple_of(
                    grp * FN1_ROWS + (kb * FN2_K1_PER_STEP + kk) * FN1_S2_PER_STEP,
                    FN1_S2_PER_STEP)
                return ref[blk, pl.ds(row0, FN1_S2_PER_STEP), :]
            return jnp.concatenate(
                [jnp.concatenate([tile(grp, blk) for blk in range(FN1_STEPS)], axis=0)
                 for grp in range(N_FOURIER_GROUPS)], axis=1).astype(BF16)

        for kk in range(FN2_K1_PER_STEP):
            x = jnp.concatenate([slab(pr_ref, kk), slab(pi_ref, kk)], axis=0)
            y_ref[:, kk, :] = _dot(w2_ref[...], x)


def _fnet(a, b, w1, w2, tcos, tsin, batch, seq):
    shape5 = (N_FOURIER_GROUPS, batch, SEQ_N1, SEQ_N2, FOURIER_GROUP)
    a5 = a.reshape(shape5)
    b5 = b.reshape(shape5)
    last1 = FN1_STEPS - 1
    in_spec = pl.BlockSpec((N_FOURIER_GROUPS, None, SEQ_N1, FN1_S2_PER_STEP, FOURIER_GROUP),
                           lambda bb, j: (0, bb, 0, jnp.minimum(j, last1), 0))
    tw_spec = pl.BlockSpec((None, SEQ_N1, FN1_S2_PER_STEP),
                           lambda bb, j: (jnp.minimum(j, last1), 0, 0))
    out_spec = pl.BlockSpec((None, SEQ_N2, FN2_K1_PER_STEP, D_FOURIER),
                            lambda bb, j: (bb, 0, jnp.maximum(j - FN1_STEPS, 0), 0))
    scratch = pltpu.VMEM((FN1_STEPS, N_FOURIER_GROUPS * FN1_ROWS, FOURIER_GROUP), F32)
    blk_bytes = N_FOURIER_GROUPS * FN1_ROWS * FOURIER_GROUP * 4
    vmem = (2 * FN1_STEPS * blk_bytes
            + 4 * blk_bytes
            + 2 * SEQ_N2 * FN2_K1_PER_STEP * D_FOURIER * 4
            + (8 << 20))
    y = pl.pallas_call(
        _fnet_kernel,
        grid=(batch, FN1_STEPS + FN2_STEPS),
        in_specs=[in_spec, in_spec, _const_spec((2 * SEQ_N1, 2 * SEQ_N1)), tw_spec, tw_spec,
                  _const_spec((SEQ_N2, 2 * SEQ_N2))],
        out_specs=out_spec,
        out_shape=jax.ShapeDtypeStruct((batch, SEQ_N2, SEQ_N1, D_FOURIER), F32),
        scratch_shapes=[scratch, scratch],
        compiler_params=_cparams(vmem, 2),
        name="fnet",
    )(a5, b5, w1, tcos, tsin, w2)
    return y.reshape(batch * seq, D_FOURIER)


def _dft_constants():
    def cs(n):
        idx = np.arange(n)
        ang = 2.0 * np.pi * ((idx[:, None] * idx[None, :]) % n) / n
        return np.cos(ang), np.sin(ang)

    cc, sc = cs(FOURIER_GROUP)
    chan = np.concatenate([cc, sc], axis=1)
    c1, s1 = cs(SEQ_N1)
    w1 = np.block([[c1, -s1], [-s1, -c1]])
    c2, s2 = cs(SEQ_N2)
    norm = 1.0 / math.sqrt(SEQ_N1 * SEQ_N2 * FOURIER_GROUP)
    w2 = np.concatenate([c2, s2], axis=1) * norm
    k1 = np.arange(SEQ_N1)
    s2i = np.arange(SEQ_N2)
    ang = 2.0 * np.pi * (k1[:, None] * s2i[None, :]) / (SEQ_N1 * SEQ_N2)
    nblk = SEQ_N2 // FN1_S2_PER_STEP

    def blocked(t):
        return t.reshape(SEQ_N1, nblk, FN1_S2_PER_STEP).transpose(1, 0, 2)

    return (jnp.asarray(chan, F32).astype(BF16), jnp.asarray(w1, F32).astype(BF16),
            jnp.asarray(w2, F32).astype(BF16),
            jnp.asarray(blocked(np.cos(ang)), F32), jnp.asarray(blocked(np.sin(ang)), F32))


def kernel(x, ffn1_w_gate, ffn1_w_up, ffn1_w_down, ln1_g, ln1_b, w_in, rpb, g_attn, g_fourier,
           w_out, ln2_g, ln2_b, ffn2_w_gate, ffn2_w_up, ffn2_w_down, ln3_g, ln3_b):
    batch, seq, d = x.shape
    assert d == D_MODEL and seq == SEQ_N1 * SEQ_N2 and seq % (ATT_PAIRS_PER_STEP * PAIR_Q) == 0
    assert rpb.shape[0] == DEPTH
    chan, w1, w2, tcos, tsin = _dft_constants()

    def row(p, l):
        return p[l].reshape(1, -1).astype(F32)

    wg1, wu1, wd1, win, wout, wg2, wu2, wd2 = _cast_weights(
        [ffn1_w_gate, ffn1_w_up, ffn1_w_down, w_in, w_out, ffn2_w_gate, ffn2_w_up, ffn2_w_down])

    h = x.reshape(batch * seq, d).astype(F32)
    for l in range(DEPTH):
        h, q, k, v, a, b = _ffn_proj(l, h, wg1, wu1, wd1, row(ln1_g, l), row(ln1_b, l), win, chan)
        attn = _nattn(q, k, v, _toeplitz_tables(rpb[l]), batch, seq)
        four = _fnet(a, b, w1, w2, tcos, tsin, batch, seq)
        h = _mix_ffn(l, attn, four, h, row(g_attn, l), row(g_fourier, l), wout,
                     row(ln2_g, l), row(ln2_b, l), wg2, wu2, wd2, row(ln3_g, l), row(ln3_b, l))
    return h.reshape(batch, seq, d).astype(x.dtype)
```

```python
import functools
import math

import jax
import jax.numpy as jnp
import numpy as np
from jax import lax
from jax.experimental import pallas as pl
from jax.experimental.pallas import tpu as pltpu

D_MODEL = 1024
DEPTH = 2
GRID_W = 64
WIN_ROWS = 8
WIN_COLS = 16
N_ATTN_HEADS = 8
HEAD_DIM = 64
D_ATTN = N_ATTN_HEADS * HEAD_DIM
N_FOURIER_GROUPS = 4
FOURIER_GROUP = 128
D_FOURIER = N_FOURIER_GROUPS * FOURIER_GROUP
D_IN = 3 * D_ATTN + D_FOURIER
D_FF = 2816
ALPHA = (2.0 * DEPTH) ** 0.25
LN_EPS = 1e-5
RMS_EPS = 1e-6
NEG_INF = -1e30

V7X_LANES = 128
V7X_MXU_DIM = 256
V7X_VMEM_BYTES = 64 * 1024 * 1024

F32 = jnp.float32
BF16 = jnp.bfloat16


def _cparams(vmem_bytes, n_axes):
    return pltpu.CompilerParams(
        dimension_semantics=("arbitrary",) * n_axes,
        vmem_limit_bytes=min(int(vmem_bytes), V7X_VMEM_BYTES - (2 << 20)),
    )


def _const_spec(shape):
    nd = len(shape)
    return pl.BlockSpec(shape, lambda *_: (0,) * nd, pipeline_mode=pl.Buffered(1))


def _layer_norm(z, g, b):
    mu = jnp.mean(z, axis=-1, keepdims=True)
    zc = z - mu
    var = jnp.mean(zc * zc, axis=-1, keepdims=True)
    return zc * lax.rsqrt(var + LN_EPS) * g + b


def _rms_norm(z, g):
    return z * lax.rsqrt(jnp.mean(z * z, axis=-1, keepdims=True) + RMS_EPS) * g


def _dot(a, b):
    return jnp.dot(a, b, preferred_element_type=F32)


TOK_TM = 512
MIX_TM = 1024
FFN_FC = V7X_MXU_DIM
Q_SCALE = HEAD_DIM ** -0.5 * math.log2(math.e)


TOK_SUB = 256
N_SUB = TOK_TM // TOK_SUB


def _sub_rows(s):
    return slice(s * TOK_SUB, (s + 1) * TOK_SUB)


def _swiglu_ln(xs, wg_ref, wu_ref, wd_ref, g_ref, b_ref, h_ref):
    xbs = [x.astype(BF16) for x in xs]
    for c in range(D_FF // FFN_FC):
        cols = slice(c * FFN_FC, (c + 1) * FFN_FC)
        for s, xb in enumerate(xbs):
            gate = _dot(xb, wg_ref[:, cols])
            up = _dot(xb, wu_ref[:, cols])
            h_ref[s, :, cols] = (gate * jax.nn.sigmoid(gate) * up).astype(BF16)
    outs = []
    for s, x in enumerate(xs):
        y = _dot(h_ref[s], wd_ref[...])
        outs.append(_layer_norm(ALPHA * x + 0.5 * y, g_ref[...], b_ref[...]))
    return outs


def _ffn_proj_kernel(x_ref, wg_ref, wu_ref, wd_ref, g_ref, b_ref, win_ref, cs_ref,
                     x1_ref, q_ref, k_ref, v_ref, a_ref, bb_ref, h_ref):
    x1s = _swiglu_ln([x_ref[_sub_rows(s), :] for s in range(N_SUB)],
                     wg_ref, wu_ref, wd_ref, g_ref, b_ref, h_ref)
    for s, x1 in enumerate(x1s):
        rows = _sub_rows(s)
        x1_ref[rows, :] = x1
        xb = x1.astype(BF16)
        q_ref[rows, :] = (_dot(xb, win_ref[:, 0:D_ATTN]) * Q_SCALE).astype(BF16)
        k_ref[rows, :] = _dot(xb, win_ref[:, D_ATTN:2 * D_ATTN]).astype(BF16)
        v_ref[rows, :] = _dot(xb, win_ref[:, 2 * D_ATTN:3 * D_ATTN]).astype(BF16)
        u = _dot(xb, win_ref[:, 3 * D_ATTN:]).astype(BF16)
        for grp in range(N_FOURIER_GROUPS):
            cols = slice(grp * FOURIER_GROUP, (grp + 1) * FOURIER_GROUP)
            ab = _dot(u[:, cols], cs_ref[...])
            a_ref[grp, rows, :] = ab[:, :FOURIER_GROUP]
            bb_ref[grp, rows, :] = ab[:, FOURIER_GROUP:]


def _mix_ffn_kernel(attn_ref, four_ref, x_ref, ga_ref, gf_ref, wo_ref, g2_ref, b2_ref,
                    wg_ref, wu_ref, wd_ref, g3_ref, b3_ref, o_ref, h_ref):
    x2s = []
    for s in range(MIX_TM // TOK_SUB):
        rows = _sub_rows(s)
        ma = _rms_norm(attn_ref[rows, :], ga_ref[...]).astype(BF16)
        mf = _rms_norm(four_ref[rows, :], gf_ref[...]).astype(BF16)
        z = _dot(ma, wo_ref[0:D_ATTN, :]) + _dot(mf, wo_ref[D_ATTN:, :])
        x2s.append(_layer_norm(ALPHA * x_ref[rows, :] + z, g2_ref[...], b2_ref[...]))
    outs = _swiglu_ln(x2s, wg_ref, wu_ref, wd_ref, g3_ref, b3_ref, h_ref)
    for s, out in enumerate(outs):
        o_ref[_sub_rows(s), :] = out


def _tok_spec(width, tm=TOK_TM):
    return pl.BlockSpec((tm, width), lambda i: (i, 0))


def _layer_spec(layer, rows, cols):
    return pl.BlockSpec((None, rows, cols), lambda i: (layer, 0, 0), pipeline_mode=pl.Buffered(1))


def _ffn_weight_specs(layer):
    return [_layer_spec(layer, D_MODEL, D_FF), _layer_spec(layer, D_MODEL, D_FF),
            _layer_spec(layer, D_FF, D_MODEL), _const_spec((1, D_MODEL)), _const_spec((1, D_MODEL))]


_FFN_VMEM = 3 * D_MODEL * D_FF * 2 + TOK_TM * D_FF * 2 + 8 * TOK_TM * D_MODEL * 4

CAST_STEPS = 16


def _cast_kernel(*refs):
    n = len(refs) // 2
    for src, dst in zip(refs[:n], refs[n:]):
        dst[...] = src[...].astype(BF16)


def _cast_weights(weights):
    flat = [w.reshape(-1, w.shape[-1]) for w in weights]
    specs = [pl.BlockSpec((f.shape[0] // CAST_STEPS, f.shape[1]), lambda i: (i, 0)) for f in flat]
    vmem = 2 * sum(f.size // CAST_STEPS * 6 for f in flat) + (4 << 20)
    outs = pl.pallas_call(
        _cast_kernel,
        grid=(CAST_STEPS,),
        in_specs=specs,
        out_specs=specs,
        out_shape=[jax.ShapeDtypeStruct(f.shape, BF16) for f in flat],
        compiler_params=_cparams(vmem, 1),
        name="cast_weights",
    )(*flat)
    return [o.reshape(w.shape) for o, w in zip(outs, weights)]


def _ffn_proj(layer, x, wg, wu, wd, g, b, w_in, cs):
    n = x.shape[0]
    half = jax.ShapeDtypeStruct((n, D_ATTN), BF16)
    four = jax.ShapeDtypeStruct((N_FOURIER_GROUPS, n, FOURIER_GROUP), F32)
    four_spec = pl.BlockSpec((N_FOURIER_GROUPS, TOK_TM, FOURIER_GROUP), lambda i: (0, i, 0))
    vmem = (_FFN_VMEM + D_MODEL * D_IN * 2
            + 2 * TOK_TM * (2 * D_MODEL * 4 + 3 * D_ATTN * 2 + 2 * D_FOURIER * 4)
            + 2 * TOK_TM * D_IN * 4)
    return pl.pallas_call(
        _ffn_proj_kernel,
        grid=(n // TOK_TM,),
        in_specs=([_tok_spec(D_MODEL)] + _ffn_weight_specs(layer)
                  + [_layer_spec(layer, D_MODEL, D_IN),
                     _const_spec((FOURIER_GROUP, 2 * FOURIER_GROUP))]),
        out_specs=[_tok_spec(D_MODEL)] + [_tok_spec(D_ATTN)] * 3 + [four_spec] * 2,
        out_shape=[jax.ShapeDtypeStruct((n, D_MODEL), F32)] + [half] * 3 + [four] * 2,
        scratch_shapes=[pltpu.VMEM((N_SUB, TOK_SUB, D_FF), BF16)],
        compiler_params=_cparams(vmem, 1),
        name="ffn_proj",
    )(x, wg, wu, wd, g, b, w_in, cs)


def _mix_ffn(layer, attn, four, x, ga, gf, wo, g2, b2, wg, wu, wd, g3, b3):
    n = x.shape[0]
    tm = MIX_TM
    vmem = (3 * D_MODEL * D_FF * 2 + D_MODEL * D_MODEL * 2 + tm * D_FF * 2
            + 2 * tm * (2 * D_MODEL * 4 + D_ATTN * 4 + D_FOURIER * 4)
            + 8 * TOK_TM * D_MODEL * 4)
    return pl.pallas_call(
        _mix_ffn_kernel,
        grid=(n // tm,),
        in_specs=([_tok_spec(D_ATTN, tm), _tok_spec(D_FOURIER, tm), _tok_spec(D_MODEL, tm),
                   _const_spec((1, D_ATTN)), _const_spec((1, D_FOURIER)),
                   _layer_spec(layer, D_MODEL, D_MODEL), _const_spec((1, D_MODEL)),
                   _const_spec((1, D_MODEL))]
                  + _ffn_weight_specs(layer)),
        out_specs=_tok_spec(D_MODEL, tm),
        out_shape=jax.ShapeDtypeStruct((n, D_MODEL), F32),
        scratch_shapes=[pltpu.VMEM((tm // TOK_SUB, TOK_SUB, D_FF), BF16)],
        compiler_params=_cparams(vmem, 1),
        name="mix_ffn",
    )(attn, four, x, ga, gf, wo, g2, b2, wg, wu, wd, g3, b3)


PAIR_Q = 2 * GRID_W
PAIR_KROWS = WIN_ROWS + 2
PAIR_K = PAIR_KROWS * GRID_W
ATT_PAIRS_PER_STEP = 16
ATT_PAIRS_PER_ITER = 4
N_TZ = 2 * WIN_ROWS
LOG2E = math.log2(math.e)
ATT_HALO = (WIN_ROWS // 2) * GRID_W
ATT_KV_WINDOW = ATT_PAIRS_PER_STEP * PAIR_Q + PAIR_K - PAIR_Q


def _kv_window_start(step, seq):
    tiles = jnp.clip(step * (ATT_PAIRS_PER_STEP * PAIR_Q // V7X_LANES) - ATT_HALO // V7X_LANES,
                     0, (seq - ATT_KV_WINDOW) // V7X_LANES)
    return tiles * V7X_LANES


def _toeplitz_tables(rpb_all):
    rpb_l = rpb_all.reshape((-1,) + rpb_all.shape[2:])
    cols = np.arange(GRID_W)
    c0 = np.clip(cols - WIN_COLS // 2, 0, GRID_W - WIN_COLS)
    in_win = (cols[None, :] >= c0[:, None]) & (cols[None, :] < c0[:, None] + WIN_COLS)
    dc_idx = np.clip(cols[None, :] - cols[:, None], -(WIN_COLS - 1), WIN_COLS - 1) + (WIN_COLS - 1)
    onehot = (dc_idx[None] == np.arange(2 * WIN_COLS - 1)[:, None, None]).astype(np.float32)
    tz = jnp.einsum("hdj,jqk->hdqk", rpb_l.astype(F32) * LOG2E, jnp.asarray(onehot),
                    precision=lax.Precision.HIGHEST)
    tz = jnp.where(in_win, tz, NEG_INF)
    tzp = jnp.pad(tz, ((0, 0), (1, 1), (0, 0), (0, 0)), constant_values=NEG_INF)
    left, right = tzp[:, :-1], tzp[:, 1:]
    masked = jnp.full_like(left, NEG_INF)
    return jnp.concatenate([jnp.concatenate([left, right], axis=-1),
                            jnp.concatenate([left, masked], axis=-1),
                            jnp.concatenate([masked, right], axis=-1)], axis=1)


def _nattn_kernel(q_ref, k_ref, v_ref, tz_ref, o_ref, s_ref, *, n_rows):
    step = pl.program_id(1)
    kv_start = _kv_window_start(step, n_rows * GRID_W)
    lane = lax.broadcasted_iota(jnp.int32, (PAIR_Q, V7X_LANES), 1)
    first_head = lane < HEAD_DIM
    ones_block = jnp.ones((PAIR_K, V7X_LANES), BF16)

    def pair_geometry(pl_idx):
        p = step * ATT_PAIRS_PER_STEP + pl_idx
        w = jnp.clip(2 * p - WIN_ROWS // 2, 0, n_rows - PAIR_KROWS)

        def tz_index(a, t):
            r = 2 * p + a
            r0 = jnp.clip(r - WIN_ROWS // 2, 0, n_rows - WIN_ROWS)
            kr = w + 2 * t
            ok0 = (kr >= r0) & (kr < r0 + WIN_ROWS)
            ok1 = (kr + 1 >= r0) & (kr + 1 < r0 + WIN_ROWS)
            i = jnp.clip(kr - r + WIN_ROWS, 0, N_TZ - 1)
            return jnp.where(ok0 & ok1, i,
                             jnp.where(ok0, N_TZ + i, jnp.where(ok1, 2 * N_TZ + i, N_TZ)))

        return dict(
            qrows=pl.ds(pl.multiple_of(pl_idx * PAIR_Q, PAIR_Q), PAIR_Q),
            krows=pl.ds(pl.multiple_of(w * GRID_W - kv_start, V7X_LANES), PAIR_K),
            tz_idx=[[tz_index(a, t) for t in range(PAIR_KROWS // 2)] for a in (0, 1)])

    def scores(geo, hp):
        lanes = slice(hp * V7X_LANES, (hp + 1) * V7X_LANES)
        qp = q_ref[geo["qrows"], lanes]
        kp = k_ref[geo["krows"], lanes]
        zero = jnp.zeros_like(qp)
        qq = jnp.concatenate([jnp.where(first_head, qp, zero),
                              jnp.where(first_head, zero, qp)], axis=0)
        return lax.dot_general(qq, kp, (((1,), (1,)), ((), ())),
                               preferred_element_type=F32)

    def finish(geo, hp, s):
        lanes = slice(hp * V7X_LANES, (hp + 1) * V7X_LANES)
        probs = []
        for e in (0, 1):
            h = 2 * hp + e
            bias = jnp.concatenate(
                [jnp.concatenate([tz_ref[h, geo["tz_idx"][a][t]] for t in range(PAIR_KROWS // 2)],
                                 axis=1) for a in (0, 1)], axis=0)
            se = s[e * PAIR_Q:(e + 1) * PAIR_Q] + bias
            m = jnp.max(se, axis=-1, keepdims=True)
            probs.append(jnp.exp2(se - m).astype(BF16))
        v_ext = jnp.concatenate([v_ref[geo["krows"], lanes], ones_block], axis=1)
        o = _dot(jnp.concatenate(probs, axis=0), v_ext)
        o = o[:, :V7X_LANES] / o[:, V7X_LANES:]
        o_ref[geo["qrows"], lanes] = jnp.where(first_head, o[:PAIR_Q], o[PAIR_Q:])

    def body(it, carry):
        geos = [pair_geometry(it * ATT_PAIRS_PER_ITER + j) for j in range(ATT_PAIRS_PER_ITER)]
        units = [(geos[j], hp) for j in range(ATT_PAIRS_PER_ITER)
                 for hp in range(N_ATTN_HEADS // 2)]
        s_cur = s_ref[...]
        for u in range(1, len(units)):
            s_next = scores(*units[u])
            finish(*units[u - 1], s_cur)
            s_cur = s_next
        nxt = jnp.minimum((it + 1) * ATT_PAIRS_PER_ITER, ATT_PAIRS_PER_STEP - 1)
        s_ref[...] = scores(pair_geometry(nxt), 0)
        finish(*units[-1], s_cur)
        return carry

    s_ref[...] = scores(pair_geometry(0), 0)
    lax.fori_loop(0, ATT_PAIRS_PER_STEP // ATT_PAIRS_PER_ITER, body, 0)


def _nattn(layer, q, k, v, tz2, batch, seq):
    n_rows = seq // GRID_W
    tq = ATT_PAIRS_PER_STEP * PAIR_Q
    steps = seq // tq
    k3 = k.reshape(batch, seq, D_ATTN)
    v3 = v.reshape(batch, seq, D_ATTN)
    vmem = (2 * 2 * ATT_KV_WINDOW * D_ATTN * 2
            + tz2.size // DEPTH * 4
            + 2 * tq * D_ATTN * (2 + 4)
            + (8 << 20))
    kv_spec = pl.BlockSpec(
        (None, pl.Element(ATT_KV_WINDOW), pl.Element(D_ATTN)),
        lambda b, i: (b, _kv_window_start(i, seq), 0))
    return pl.pallas_call(
        functools.partial(_nattn_kernel, n_rows=n_rows),
        grid=(batch, steps),
        in_specs=[
            pl.BlockSpec((tq, D_ATTN), lambda b, i: (b * steps + i, 0)),
            kv_spec,
            kv_spec,
            pl.BlockSpec((N_ATTN_HEADS,) + tz2.shape[1:], lambda b, i: (layer, 0, 0, 0),
                         pipeline_mode=pl.Buffered(1)),
        ],
        out_specs=pl.BlockSpec((tq, D_ATTN), lambda b, i: (b * steps + i, 0)),
        out_shape=jax.ShapeDtypeStruct((batch * seq, D_ATTN), F32),
        scratch_shapes=[pltpu.VMEM((2 * PAIR_Q, PAIR_K), F32)],
        compiler_params=_cparams(vmem, 2),
        name="nattn",
    )(q, k3, v3, tz2)


SEQ_N1 = 128
SEQ_N2 = 64
FN1_S2_PER_STEP = 8
FN2_K1_PER_STEP = 32
FN1_ROWS = SEQ_N1 * FN1_S2_PER_STEP


FN1_STEPS = SEQ_N2 // FN1_S2_PER_STEP
FN2_STEPS = SEQ_N1 // FN2_K1_PER_STEP


def _fnet_kernel(a_ref, b_ref, w1_ref, tc_ref, ts_ref, w2_ref, y_ref, pr_ref, pi_ref):
    j = pl.program_id(1)

    @pl.when(j < FN1_STEPS)
    def _stage1():
        flat = (N_FOURIER_GROUPS * FN1_ROWS, FOURIER_GROUP)
        a2, b2 = a_ref.reshape(*flat), b_ref.reshape(*flat)
        tcos = tc_ref[...]
        tsin = ts_ref[...]

        def rows(grp, jj):
            return pl.ds(grp * FN1_ROWS + jj, SEQ_N1, stride=FN1_S2_PER_STEP)

        def gather(ref2, jj):
            return jnp.concatenate([ref2[rows(grp, jj), :] for grp in range(N_FOURIER_GROUPS)],
                                   axis=1)

        for jj in range(FN1_S2_PER_STEP):
            x = jnp.concatenate([gather(a2, jj).astype(BF16), gather(b2, jj).astype(BF16)],
                                axis=0)
            t = _dot(w1_ref[...], x)
            tr = t[:SEQ_N1]
            ti = t[SEQ_N1:]
            c = tcos[:, jj:jj + 1]
            s = tsin[:, jj:jj + 1]
            pr = tr * c + ti * s
            pi = ti * c - tr * s
            for grp in range(N_FOURIER_GROUPS):
                cols = slice(grp * FOURIER_GROUP, (grp + 1) * FOURIER_GROUP)
                pr_ref[j, rows(grp, jj), :] = pr[:, cols]
                pi_ref[j, rows(grp, jj), :] = pi[:, cols]

    @pl.when(j >= FN1_STEPS)
    def _stage2():
        kb = j - FN1_STEPS

        def slab(ref, kk):
            def tile(grp, blk):
                row0 = pl.multiple_of(
                    grp * FN1_ROWS + (kb * FN2_K1_PER_STEP + kk) * FN1_S2_PER_STEP,
                    FN1_S2_PER_STEP)
                return ref[blk, pl.ds(row0, FN1_S2_PER_STEP), :]
            return jnp.concatenate(
                [jnp.concatenate([tile(grp, blk) for blk in range(FN1_STEPS)], axis=0)
                 for grp in range(N_FOURIER_GROUPS)], axis=1).astype(BF16)

        for kk in range(FN2_K1_PER_STEP):
            x = jnp.concatenate([slab(pr_ref, kk), slab(pi_ref, kk)], axis=0)
            y_ref[:, kk, :] = _dot(w2_ref[...], x)


def _fnet(a, b, w1, w2, tcos, tsin, batch, seq):
    shape5 = (N_FOURIER_GROUPS, batch, SEQ_N1, SEQ_N2, FOURIER_GROUP)
    a5 = a.reshape(shape5)
    b5 = b.reshape(shape5)
    last1 = FN1_STEPS - 1
    in_spec = pl.BlockSpec((N_FOURIER_GROUPS, None, SEQ_N1, FN1_S2_PER_STEP, FOURIER_GROUP),
                           lambda bb, j: (0, bb, 0, jnp.minimum(j, last1), 0))
    tw_spec = pl.BlockSpec((None, SEQ_N1, FN1_S2_PER_STEP),
                           lambda bb, j: (jnp.minimum(j, last1), 0, 0))
    out_spec = pl.BlockSpec((None, SEQ_N2, FN2_K1_PER_STEP, D_FOURIER),
                            lambda bb, j: (bb, 0, jnp.maximum(j - FN1_STEPS, 0), 0))
    scratch = pltpu.VMEM((FN1_STEPS, N_FOURIER_GROUPS * FN1_ROWS, FOURIER_GROUP), F32)
    blk_bytes = N_FOURIER_GROUPS * FN1_ROWS * FOURIER_GROUP * 4
    vmem = (2 * FN1_STEPS * blk_bytes
            + 4 * blk_bytes
            + 2 * SEQ_N2 * FN2_K1_PER_STEP * D_FOURIER * 4
            + (8 << 20))
    y = pl.pallas_call(
        _fnet_kernel,
        grid=(batch, FN1_STEPS + FN2_STEPS),
        in_specs=[in_spec, in_spec, _const_spec((2 * SEQ_N1, 2 * SEQ_N1)), tw_spec, tw_spec,
                  _const_spec((SEQ_N2, 2 * SEQ_N2))],
        out_specs=out_spec,
        out_shape=jax.ShapeDtypeStruct((batch, SEQ_N2, SEQ_N1, D_FOURIER), F32),
        scratch_shapes=[scratch, scratch],
        compiler_params=_cparams(vmem, 2),
        name="fnet",
    )(a5, b5, w1, tcos, tsin, w2)
    return y.reshape(batch * seq, D_FOURIER)


def _dft_constants():
    def cs(n):
        idx = np.arange(n)
        ang = 2.0 * np.pi * ((idx[:, None] * idx[None, :]) % n) / n
        return np.cos(ang), np.sin(ang)

    cc, sc = cs(FOURIER_GROUP)
    chan = np.concatenate([cc, sc], axis=1)
    c1, s1 = cs(SEQ_N1)
    w1 = np.block([[c1, -s1], [-s1, -c1]])
    c2, s2 = cs(SEQ_N2)
    norm = 1.0 / math.sqrt(SEQ_N1 * SEQ_N2 * FOURIER_GROUP)
    w2 = np.concatenate([c2, s2], axis=1) * norm
    k1 = np.arange(SEQ_N1)
    s2i = np.arange(SEQ_N2)
    ang = 2.0 * np.pi * (k1[:, None] * s2i[None, :]) / (SEQ_N1 * SEQ_N2)
    nblk = SEQ_N2 // FN1_S2_PER_STEP

    def blocked(t):
        return t.reshape(SEQ_N1, nblk, FN1_S2_PER_STEP).transpose(1, 0, 2)

    return (jnp.asarray(chan, F32).astype(BF16), jnp.asarray(w1, F32).astype(BF16),
            jnp.asarray(w2, F32).astype(BF16),
            jnp.asarray(blocked(np.cos(ang)), F32), jnp.asarray(blocked(np.sin(ang)), F32))


def kernel(x, ffn1_w_gate, ffn1_w_up, ffn1_w_down, ln1_g, ln1_b, w_in, rpb, g_attn, g_fourier,
           w_out, ln2_g, ln2_b, ffn2_w_gate, ffn2_w_up, ffn2_w_down, ln3_g, ln3_b):
    batch, seq, d = x.shape
    assert d == D_MODEL and seq == SEQ_N1 * SEQ_N2 and seq % (ATT_PAIRS_PER_STEP * PAIR_Q) == 0
    assert rpb.shape[0] == DEPTH
    chan, w1, w2, tcos, tsin = _dft_constants()

    def row(p, l):
        return p[l].reshape(1, -1).astype(F32)

    tz2 = _toeplitz_tables(rpb)
    wg1, wu1, wd1, win, wout, wg2, wu2, wd2 = _cast_weights(
        [ffn1_w_gate, ffn1_w_up, ffn1_w_down, w_in, w_out, ffn2_w_gate, ffn2_w_up, ffn2_w_down])

    h = x.reshape(batch * seq, d).astype(F32)
    for l in range(DEPTH):
        h, q, k, v, a, b = _ffn_proj(l, h, wg1, wu1, wd1, row(ln1_g, l), row(ln1_b, l), win, chan)
        attn = _nattn(l, q, k, v, tz2, batch, seq)
        four = _fnet(a, b, w1, w2, tcos, tsin, batch, seq)
        h = _mix_ffn(l, attn, four, h, row(g_attn, l), row(g_fourier, l), wout,
                     row(ln2_g, l), row(ln2_b, l), wg2, wu2, wd2, row(ln3_g, l), row(ln3_b, l))
    return h.reshape(batch, seq, d).astype(x.dtype)
```

```python
import functools
import math

import jax
import jax.numpy as jnp
import numpy as np
from jax import lax
from jax.experimental import pallas as pl
from jax.experimental.pallas import tpu as pltpu

D_MODEL = 1024
DEPTH = 2
GRID_W = 64
WIN_ROWS = 8
WIN_COLS = 16
N_ATTN_HEADS = 8
HEAD_DIM = 64
D_ATTN = N_ATTN_HEADS * HEAD_DIM
N_FOURIER_GROUPS = 4
FOURIER_GROUP = 128
D_FOURIER = N_FOURIER_GROUPS * FOURIER_GROUP
D_IN = 3 * D_ATTN + D_FOURIER
D_FF = 2816
ALPHA = (2.0 * DEPTH) ** 0.25
LN_EPS = 1e-5
RMS_EPS = 1e-6
NEG_INF = -1e30

V7X_LANES = 128
V7X_MXU_DIM = 256
V7X_VMEM_BYTES = 64 * 1024 * 1024

F32 = jnp.float32
BF16 = jnp.bfloat16


def _cparams(vmem_bytes, n_axes):
    return pltpu.CompilerParams(
        dimension_semantics=("arbitrary",) * n_axes,
        vmem_limit_bytes=min(int(vmem_bytes), V7X_VMEM_BYTES - (2 << 20)),
    )


def _const_spec(shape):
    nd = len(shape)
    return pl.BlockSpec(shape, lambda *_: (0,) * nd, pipeline_mode=pl.Buffered(1))


def _layer_norm(z, g, b):
    mu = jnp.mean(z, axis=-1, keepdims=True)
    zc = z - mu
    var = jnp.mean(zc * zc, axis=-1, keepdims=True)
    return zc * lax.rsqrt(var + LN_EPS) * g + b


def _rms_norm(z, g):
    return z * lax.rsqrt(jnp.mean(z * z, axis=-1, keepdims=True) + RMS_EPS) * g


def _dot(a, b):
    return jnp.dot(a, b, preferred_element_type=F32)


TOK_TM = 1024
MIX_TM = 1024
FFN_FC = V7X_MXU_DIM
Q_SCALE = HEAD_DIM ** -0.5 * math.log2(math.e)


TOK_SUB = 256
N_SUB = TOK_TM // TOK_SUB


def _sub_rows(s):
    return slice(s * TOK_SUB, (s + 1) * TOK_SUB)


def _swiglu_ln(xs, wg_ref, wu_ref, wd_ref, g_ref, b_ref, h_ref):
    xbs = [x.astype(BF16) for x in xs]
    for c in range(D_FF // FFN_FC):
        cols = slice(c * FFN_FC, (c + 1) * FFN_FC)
        for s, xb in enumerate(xbs):
            gate = _dot(xb, wg_ref[:, cols])
            up = _dot(xb, wu_ref[:, cols])
            h_ref[s, :, cols] = (gate * jax.nn.sigmoid(gate) * up).astype(BF16)
    outs = []
    for s, x in enumerate(xs):
        y = _dot(h_ref[s], wd_ref[...])
        outs.append(_layer_norm(ALPHA * x + 0.5 * y, g_ref[...], b_ref[...]))
    return outs


def _ffn_proj_kernel(x_ref, wg_ref, wu_ref, wd_ref, g_ref, b_ref, win_ref,
                     x1_ref, q_ref, k_ref, v_ref, u_ref, h_ref):
    x1s = _swiglu_ln([x_ref[_sub_rows(s), :] for s in range(N_SUB)],
                     wg_ref, wu_ref, wd_ref, g_ref, b_ref, h_ref)
    for s, x1 in enumerate(x1s):
        rows = _sub_rows(s)
        x1_ref[rows, :] = x1
        xb = x1.astype(BF16)
        q_ref[rows, :] = (_dot(xb, win_ref[:, 0:D_ATTN]) * Q_SCALE).astype(BF16)
        k_ref[rows, :] = _dot(xb, win_ref[:, D_ATTN:2 * D_ATTN]).astype(BF16)
        v_ref[rows, :] = _dot(xb, win_ref[:, 2 * D_ATTN:3 * D_ATTN]).astype(BF16)
        u = _dot(xb, win_ref[:, 3 * D_ATTN:])
        for grp in range(N_FOURIER_GROUPS):
            u_ref[grp, rows, :] = u[:, grp * FOURIER_GROUP:(grp + 1) * FOURIER_GROUP]


def _mix_ffn_kernel(attn_ref, four_ref, x_ref, ga_ref, gf_ref, wo_ref, g2_ref, b2_ref,
                    wg_ref, wu_ref, wd_ref, g3_ref, b3_ref, o_ref, h_ref):
    x2s = []
    for s in range(MIX_TM // TOK_SUB):
        rows = _sub_rows(s)
        ma = _rms_norm(attn_ref[rows, :], ga_ref[...]).astype(BF16)
        mf = _rms_norm(four_ref[rows, :], gf_ref[...]).astype(BF16)
        z = _dot(ma, wo_ref[0:D_ATTN, :]) + _dot(mf, wo_ref[D_ATTN:, :])
        x2s.append(_layer_norm(ALPHA * x_ref[rows, :] + z, g2_ref[...], b2_ref[...]))
    outs = _swiglu_ln(x2s, wg_ref, wu_ref, wd_ref, g3_ref, b3_ref, h_ref)
    for s, out in enumerate(outs):
        o_ref[_sub_rows(s), :] = out


def _tok_spec(width, tm=TOK_TM):
    return pl.BlockSpec((tm, width), lambda i: (i, 0))


def _layer_spec(layer, rows, cols):
    return pl.BlockSpec((None, rows, cols), lambda i: (layer, 0, 0), pipeline_mode=pl.Buffered(1))


def _ffn_weight_specs(layer):
    return [_layer_spec(layer, D_MODEL, D_FF), _layer_spec(layer, D_MODEL, D_FF),
            _layer_spec(layer, D_FF, D_MODEL), _const_spec((1, D_MODEL)), _const_spec((1, D_MODEL))]


CAST_STEPS = 16


def _cast_kernel(*refs):
    n = len(refs) // 2
    for src, dst in zip(refs[:n], refs[n:]):
        dst[...] = src[...].astype(BF16)


def _cast_weights(weights):
    flat = [w.reshape(-1, w.shape[-1]) for w in weights]
    specs = [pl.BlockSpec((f.shape[0] // CAST_STEPS, f.shape[1]), lambda i: (i, 0)) for f in flat]
    vmem = 2 * sum(f.size // CAST_STEPS * 6 for f in flat) + (4 << 20)
    outs = pl.pallas_call(
        _cast_kernel,
        grid=(CAST_STEPS,),
        in_specs=specs,
        out_specs=specs,
        out_shape=[jax.ShapeDtypeStruct(f.shape, BF16) for f in flat],
        compiler_params=_cparams(vmem, 1),
        name="cast_weights",
    )(*flat)
    return [o.reshape(w.shape) for o, w in zip(outs, weights)]


def _ffn_proj(layer, x, wg, wu, wd, g, b, w_in):
    n = x.shape[0]
    half = jax.ShapeDtypeStruct((n, D_ATTN), BF16)
    four = jax.ShapeDtypeStruct((N_FOURIER_GROUPS, n, FOURIER_GROUP), F32)
    four_spec = pl.BlockSpec((N_FOURIER_GROUPS, TOK_TM, FOURIER_GROUP), lambda i: (0, i, 0))
    vmem = (3 * D_MODEL * D_FF * 2 + D_MODEL * D_IN * 2 + TOK_TM * D_FF * 2
            + 2 * TOK_TM * (2 * D_MODEL * 4 + 3 * D_ATTN * 2 + D_FOURIER * 4)
            + 8 * TOK_SUB * D_MODEL * 4)
    return pl.pallas_call(
        _ffn_proj_kernel,
        grid=(n // TOK_TM,),
        in_specs=([_tok_spec(D_MODEL)] + _ffn_weight_specs(layer)
                  + [_layer_spec(layer, D_MODEL, D_IN)]),
        out_specs=[_tok_spec(D_MODEL)] + [_tok_spec(D_ATTN)] * 3 + [four_spec],
        out_shape=[jax.ShapeDtypeStruct((n, D_MODEL), F32)] + [half] * 3 + [four],
        scratch_shapes=[pltpu.VMEM((N_SUB, TOK_SUB, D_FF), BF16)],
        compiler_params=_cparams(vmem, 1),
        name="ffn_proj",
    )(x, wg, wu, wd, g, b, w_in)


def _mix_ffn(layer, attn, four, x, ga, gf, wo, g2, b2, wg, wu, wd, g3, b3):
    n = x.shape[0]
    tm = MIX_TM
    vmem = (3 * D_MODEL * D_FF * 2 + D_MODEL * D_MODEL * 2 + tm * D_FF * 2
            + 2 * tm * (2 * D_MODEL * 4 + D_ATTN * 4 + D_FOURIER * 4)
            + 16 * TOK_SUB * D_MODEL * 4)
    return pl.pallas_call(
        _mix_ffn_kernel,
        grid=(n // tm,),
        in_specs=([_tok_spec(D_ATTN, tm), _tok_spec(D_FOURIER, tm), _tok_spec(D_MODEL, tm),
                   _const_spec((1, D_ATTN)), _const_spec((1, D_FOURIER)),
                   _layer_spec(layer, D_MODEL, D_MODEL), _const_spec((1, D_MODEL)),
                   _const_spec((1, D_MODEL))]
                  + _ffn_weight_specs(layer)),
        out_specs=_tok_spec(D_MODEL, tm),
        out_shape=jax.ShapeDtypeStruct((n, D_MODEL), F32),
        scratch_shapes=[pltpu.VMEM((tm // TOK_SUB, TOK_SUB, D_FF), BF16)],
        compiler_params=_cparams(vmem, 1),
        name="mix_ffn",
    )(attn, four, x, ga, gf, wo, g2, b2, wg, wu, wd, g3, b3)


PAIR_Q = 2 * GRID_W
PAIR_KROWS = WIN_ROWS + 2
PAIR_K = PAIR_KROWS * GRID_W
ATT_PAIRS_PER_STEP = 16
ATT_PAIRS_PER_ITER = 4
N_TZ = 2 * WIN_ROWS
LOG2E = math.log2(math.e)
ATT_HALO = (WIN_ROWS // 2) * GRID_W
ATT_KV_WINDOW = ATT_PAIRS_PER_STEP * PAIR_Q + PAIR_K - PAIR_Q


def _kv_window_start(step, seq):
    tiles = jnp.clip(step * (ATT_PAIRS_PER_STEP * PAIR_Q // V7X_LANES) - ATT_HALO // V7X_LANES,
                     0, (seq - ATT_KV_WINDOW) // V7X_LANES)
    return tiles * V7X_LANES


def _toeplitz_tables(rpb_l):
    cols = np.arange(GRID_W)
    c0 = np.clip(cols - WIN_COLS // 2, 0, GRID_W - WIN_COLS)
    in_win = (cols[None, :] >= c0[:, None]) & (cols[None, :] < c0[:, None] + WIN_COLS)
    dc_idx = np.clip(cols[None, :] - cols[:, None], -(WIN_COLS - 1), WIN_COLS - 1) + (WIN_COLS - 1)
    onehot = (dc_idx[None] == np.arange(2 * WIN_COLS - 1)[:, None, None]).astype(np.float32)
    tz = jnp.einsum("hdj,jqk->hdqk", rpb_l.astype(F32) * LOG2E, jnp.asarray(onehot),
                    precision=lax.Precision.HIGHEST)
    tz = jnp.where(in_win, tz, NEG_INF)
    tzp = jnp.pad(tz, ((0, 0), (1, 1), (0, 0), (0, 0)), constant_values=NEG_INF)
    left, right = tzp[:, :-1], tzp[:, 1:]
    masked = jnp.full_like(left, NEG_INF)
    return jnp.concatenate([jnp.concatenate([left, right], axis=-1),
                            jnp.concatenate([left, masked], axis=-1),
                            jnp.concatenate([masked, right], axis=-1)], axis=1)


def _nattn_kernel(q_ref, k_ref, v_ref, tz_ref, o_ref, s_ref, *, n_rows):
    step = pl.program_id(1)
    kv_start = _kv_window_start(step, n_rows * GRID_W)
    lane = lax.broadcasted_iota(jnp.int32, (PAIR_Q, V7X_LANES), 1)
    first_head = lane < HEAD_DIM
    ones_block = jnp.ones((PAIR_K, V7X_LANES), BF16)

    def pair_geometry(pl_idx):
        p = step * ATT_PAIRS_PER_STEP + pl_idx
        w = jnp.clip(2 * p - WIN_ROWS // 2, 0, n_rows - PAIR_KROWS)

        def tz_index(a, t):
            r = 2 * p + a
            r0 = jnp.clip(r - WIN_ROWS // 2, 0, n_rows - WIN_ROWS)
            kr = w + 2 * t
            ok0 = (kr >= r0) & (kr < r0 + WIN_ROWS)
            ok1 = (kr + 1 >= r0) & (kr + 1 < r0 + WIN_ROWS)
            i = jnp.clip(kr - r + WIN_ROWS, 0, N_TZ - 1)
            return jnp.where(ok0 & ok1, i,
                             jnp.where(ok0, N_TZ + i, jnp.where(ok1, 2 * N_TZ + i, N_TZ)))

        return dict(
            qrows=pl.ds(pl.multiple_of(pl_idx * PAIR_Q, PAIR_Q), PAIR_Q),
            krows=pl.ds(pl.multiple_of(w * GRID_W - kv_start, V7X_LANES), PAIR_K),
            tz_idx=[[tz_index(a, t) for t in range(PAIR_KROWS // 2)] for a in (0, 1)])

    def scores(geo, hp):
        lanes = slice(hp * V7X_LANES, (hp + 1) * V7X_LANES)
        qp = q_ref[geo["qrows"], lanes]
        kp = k_ref[geo["krows"], lanes]
        zero = jnp.zeros_like(qp)
        qq = jnp.concatenate([jnp.where(first_head, qp, zero),
                              jnp.where(first_head, zero, qp)], axis=0)
        return lax.dot_general(qq, kp, (((1,), (1,)), ((), ())),
                               preferred_element_type=F32)

    def finish(geo, hp, s):
        lanes = slice(hp * V7X_LANES, (hp + 1) * V7X_LANES)
        probs = []
        for e in (0, 1):
            h = 2 * hp + e
            bias = jnp.concatenate(
                [jnp.concatenate([tz_ref[h, geo["tz_idx"][a][t]] for t in range(PAIR_KROWS // 2)],
                                 axis=1) for a in (0, 1)], axis=0)
            se = s[e * PAIR_Q:(e + 1) * PAIR_Q] + bias
            m = jnp.max(se, axis=-1, keepdims=True)
            probs.append(jnp.exp2(se - m).astype(BF16))
        v_ext = jnp.concatenate([v_ref[geo["krows"], lanes], ones_block], axis=1)
        o = _dot(jnp.concatenate(probs, axis=0), v_ext)
        o = o[:, :V7X_LANES] / o[:, V7X_LANES:]
        o_ref[geo["qrows"], lanes] = jnp.where(first_head, o[:PAIR_Q], o[PAIR_Q:])

    def body(it, carry):
        geos = [pair_geometry(it * ATT_PAIRS_PER_ITER + j) for j in range(ATT_PAIRS_PER_ITER)]
        units = [(geos[j], hp) for j in range(ATT_PAIRS_PER_ITER)
                 for hp in range(N_ATTN_HEADS // 2)]
        s_cur = s_ref[...]
        for u in range(1, len(units)):
            s_next = scores(*units[u])
            finish(*units[u - 1], s_cur)
            s_cur = s_next
        nxt = jnp.minimum((it + 1) * ATT_PAIRS_PER_ITER, ATT_PAIRS_PER_STEP - 1)
        s_ref[...] = scores(pair_geometry(nxt), 0)
        finish(*units[-1], s_cur)
        return carry

    s_ref[...] = scores(pair_geometry(0), 0)
    lax.fori_loop(0, ATT_PAIRS_PER_STEP // ATT_PAIRS_PER_ITER, body, 0)


def _nattn(q, k, v, tz2, batch, seq):
    n_rows = seq // GRID_W
    tq = ATT_PAIRS_PER_STEP * PAIR_Q
    steps = seq // tq
    k3 = k.reshape(batch, seq, D_ATTN)
    v3 = v.reshape(batch, seq, D_ATTN)
    vmem = (2 * 2 * ATT_KV_WINDOW * D_ATTN * 2
            + tz2.size * 4
            + 2 * tq * D_ATTN * (2 + 4)
            + (8 << 20))
    kv_spec = pl.BlockSpec(
        (None, pl.Element(ATT_KV_WINDOW), pl.Element(D_ATTN)),
        lambda b, i: (b, _kv_window_start(i, seq), 0))
    return pl.pallas_call(
        functools.partial(_nattn_kernel, n_rows=n_rows),
        grid=(batch, steps),
        in_specs=[
            pl.BlockSpec((tq, D_ATTN), lambda b, i: (b * steps + i, 0)),
            kv_spec,
            kv_spec,
            _const_spec(tz2.shape),
        ],
        out_specs=pl.BlockSpec((tq, D_ATTN), lambda b, i: (b * steps + i, 0)),
        out_shape=jax.ShapeDtypeStruct((batch * seq, D_ATTN), F32),
        scratch_shapes=[pltpu.VMEM((2 * PAIR_Q, PAIR_K), F32)],
        compiler_params=_cparams(vmem, 2),
        name="nattn",
    )(q, k3, v3, tz2)


SEQ_N1 = 128
SEQ_N2 = 64
FN1_S2_PER_STEP = 8
FN2_K1_PER_STEP = 32
FN1_ROWS = SEQ_N1 * FN1_S2_PER_STEP


FN1_STEPS = SEQ_N2 // FN1_S2_PER_STEP
P_SLAB_PITCH = SEQ_N1 + 8
FN2_STEPS = SEQ_N1 // FN2_K1_PER_STEP


def _fnet_kernel(u_ref, cs_ref, w1_ref, tc_ref, ts_ref, w2_ref, y_ref, pr_ref, pi_ref):
    j = pl.program_id(1)

    @pl.when(j < FN1_STEPS)
    def _stage1():
        u2 = u_ref.reshape(N_FOURIER_GROUPS * FN1_ROWS, FOURIER_GROUP)
        tcos = tc_ref[...]
        tsin = ts_ref[...]

        def rows(grp, jj):
            return pl.ds(grp * FN1_ROWS + jj, SEQ_N1, stride=FN1_S2_PER_STEP)

        for jj in range(FN1_S2_PER_STEP):
            ab = [_dot(u2[rows(grp, jj), :].astype(BF16), cs_ref[...])
                  for grp in range(N_FOURIER_GROUPS)]
            x = jnp.concatenate(
                [jnp.concatenate([t[:, :FOURIER_GROUP] for t in ab], axis=1),
                 jnp.concatenate([t[:, FOURIER_GROUP:] for t in ab], axis=1)],
                axis=0).astype(BF16)
            t = _dot(w1_ref[...], x)
            tr = t[:SEQ_N1]
            ti = t[SEQ_N1:]
            c = tcos[:, jj:jj + 1]
            s = tsin[:, jj:jj + 1]
            pr = tr * c + ti * s
            pi = ti * c - tr * s
            for grp in range(N_FOURIER_GROUPS):
                cols = slice(grp * FOURIER_GROUP, (grp + 1) * FOURIER_GROUP)
                slab0 = (grp * FN1_S2_PER_STEP + jj) * P_SLAB_PITCH
                pr_ref[j, pl.ds(slab0, SEQ_N1), :] = pr[:, cols]
                pi_ref[j, pl.ds(slab0, SEQ_N1), :] = pi[:, cols]

    @pl.when(j >= FN1_STEPS)
    def _stage2():
        kb = j - FN1_STEPS

        def slab(ref, kk):
            def tile(grp, blk):
                row0 = grp * FN1_S2_PER_STEP * P_SLAB_PITCH + kb * FN2_K1_PER_STEP + kk
                return ref[blk, pl.ds(row0, FN1_S2_PER_STEP, stride=P_SLAB_PITCH), :]
            return jnp.concatenate(
                [jnp.concatenate([tile(grp, blk) for blk in range(FN1_STEPS)], axis=0)
                 for grp in range(N_FOURIER_GROUPS)], axis=1).astype(BF16)

        for kk in range(FN2_K1_PER_STEP):
            x = jnp.concatenate([slab(pr_ref, kk), slab(pi_ref, kk)], axis=0)
            y_ref[:, kk, :] = _dot(w2_ref[...], x)


def _fnet(u, chan, w1, w2, tcos, tsin, batch, seq):
    shape5 = (N_FOURIER_GROUPS, batch, SEQ_N1, SEQ_N2, FOURIER_GROUP)
    u5 = u.reshape(shape5)
    last1 = FN1_STEPS - 1
    in_spec = pl.BlockSpec((N_FOURIER_GROUPS, None, SEQ_N1, FN1_S2_PER_STEP, FOURIER_GROUP),
                           lambda bb, j: (0, bb, 0, jnp.minimum(j, last1), 0))
    tw_spec = pl.BlockSpec((None, SEQ_N1, FN1_S2_PER_STEP),
                           lambda bb, j: (jnp.minimum(j, last1), 0, 0))
    out_spec = pl.BlockSpec((None, SEQ_N2, FN2_K1_PER_STEP, D_FOURIER),
                            lambda bb, j: (bb, 0, jnp.maximum(j - FN1_STEPS, 0), 0))
    scratch_rows = N_FOURIER_GROUPS * FN1_S2_PER_STEP * P_SLAB_PITCH
    scratch = pltpu.VMEM((FN1_STEPS, scratch_rows, FOURIER_GROUP), F32)
    blk_bytes = N_FOURIER_GROUPS * FN1_ROWS * FOURIER_GROUP * 4
    vmem = (2 * FN1_STEPS * scratch_rows * FOURIER_GROUP * 4
            + 2 * blk_bytes
            + 2 * SEQ_N2 * FN2_K1_PER_STEP * D_FOURIER * 4
            + (8 << 20))
    y = pl.pallas_call(
        _fnet_kernel,
        grid=(batch, FN1_STEPS + FN2_STEPS),
        in_specs=[in_spec, _const_spec((FOURIER_GROUP, 2 * FOURIER_GROUP)),
                  _const_spec((2 * SEQ_N1, 2 * SEQ_N1)), tw_spec, tw_spec,
                  _const_spec((SEQ_N2, 2 * SEQ_N2))],
        out_specs=out_spec,
        out_shape=jax.ShapeDtypeStruct((batch, SEQ_N2, SEQ_N1, D_FOURIER), F32),
        scratch_shapes=[scratch, scratch],
        compiler_params=_cparams(vmem, 2),
        name="fnet",
    )(u5, chan, w1, tcos, tsin, w2)
    return y.reshape(batch * seq, D_FOURIER)


def _dft_constants():
    def cs(n):
        idx = np.arange(n)
        ang = 2.0 * np.pi * ((idx[:, None] * idx[None, :]) % n) / n
        return np.cos(ang), np.sin(ang)

    cc, sc = cs(FOURIER_GROUP)
    chan = np.concatenate([cc, sc], axis=1)
    c1, s1 = cs(SEQ_N1)
    w1 = np.block([[c1, -s1], [-s1, -c1]])
    c2, s2 = cs(SEQ_N2)
    norm = 1.0 / math.sqrt(SEQ_N1 * SEQ_N2 * FOURIER_GROUP)
    w2 = np.concatenate([c2, s2], axis=1) * norm
    k1 = np.arange(SEQ_N1)
    s2i = np.arange(SEQ_N2)
    ang = 2.0 * np.pi * (k1[:, None] * s2i[None, :]) / (SEQ_N1 * SEQ_N2)
    nblk = SEQ_N2 // FN1_S2_PER_STEP

    def blocked(t):
        return t.reshape(SEQ_N1, nblk, FN1_S2_PER_STEP).transpose(1, 0, 2)

    return (jnp.asarray(chan, F32).astype(BF16), jnp.asarray(w1, F32).astype(BF16),
            jnp.asarray(w2, F32).astype(BF16),
            jnp.asarray(blocked(np.cos(ang)), F32), jnp.asarray(blocked(np.sin(ang)), F32))


def kernel(x, ffn1_w_gate, ffn1_w_up, ffn1_w_down, ln1_g, ln1_b, w_in, rpb, g_attn, g_fourier,
           w_out, ln2_g, ln2_b, ffn2_w_gate, ffn2_w_up, ffn2_w_down, ln3_g, ln3_b):
    batch, seq, d = x.shape
    assert d == D_MODEL and seq == SEQ_N1 * SEQ_N2 and seq % (ATT_PAIRS_PER_STEP * PAIR_Q) == 0
    assert rpb.shape[0] == DEPTH
    chan, w1, w2, tcos, tsin = _dft_constants()

    def row(p, l):
        return p[l].reshape(1, -1).astype(F32)

    wg1, wu1, wd1, win, wout, wg2, wu2, wd2 = _cast_weights(
        [ffn1_w_gate, ffn1_w_up, ffn1_w_down, w_in, w_out, ffn2_w_gate, ffn2_w_up, ffn2_w_down])

    h = x.reshape(batch * seq, d).astype(F32)
    for l in range(DEPTH):
        h, q, k, v, u = _ffn_proj(l, h, wg1, wu1, wd1, row(ln1_g, l), row(ln1_b, l), win)
        attn = _nattn(q, k, v, _toeplitz_tables(rpb[l]), batch, seq)
        four = _fnet(u, chan, w1, w2, tcos, tsin, batch, seq)
        h = _mix_ffn(l, attn, four, h, row(g_attn, l), row(g_fourier, l), wout,
                     row(ln2_g, l), row(ln2_b, l), wg2, wu2, wd2, row(ln3_g, l), row(ln3_b, l))
    return h.reshape(batch, seq, d).astype(x.dtype)
```

```python
import functools
import math

import jax
import jax.numpy as jnp
import numpy as np
from jax import lax
from jax.experimental import pallas as pl
from jax.experimental.pallas import tpu as pltpu

D_MODEL = 1024
DEPTH = 2
GRID_W = 64
WIN_ROWS = 8
WIN_COLS = 16
N_ATTN_HEADS = 8
HEAD_DIM = 64
D_ATTN = N_ATTN_HEADS * HEAD_DIM
N_FOURIER_GROUPS = 4
FOURIER_GROUP = 128
D_FOURIER = N_FOURIER_GROUPS * FOURIER_GROUP
D_IN = 3 * D_ATTN + D_FOURIER
D_FF = 2816
ALPHA = (2.0 * DEPTH) ** 0.25
LN_EPS = 1e-5
RMS_EPS = 1e-6
NEG_INF = -1e30

V7X_LANES = 128
V7X_MXU_DIM = 256
V7X_VMEM_BYTES = 64 * 1024 * 1024

F32 = jnp.float32
BF16 = jnp.bfloat16


def _cparams(vmem_bytes, n_axes):
    return pltpu.CompilerParams(
        dimension_semantics=("arbitrary",) * n_axes,
        vmem_limit_bytes=min(int(vmem_bytes), V7X_VMEM_BYTES - (2 << 20)),
    )


def _const_spec(shape):
    nd = len(shape)
    return pl.BlockSpec(shape, lambda *_: (0,) * nd, pipeline_mode=pl.Buffered(1))


def _layer_norm(z, g, b):
    mu = jnp.mean(z, axis=-1, keepdims=True)
    zc = z - mu
    var = jnp.mean(zc * zc, axis=-1, keepdims=True)
    return zc * lax.rsqrt(var + LN_EPS) * g + b


def _rms_norm(z, g):
    return z * lax.rsqrt(jnp.mean(z * z, axis=-1, keepdims=True) + RMS_EPS) * g


def _dot(a, b):
    return jnp.dot(a, b, preferred_element_type=F32)


TOK_TM = 1024
MIX_TM = 1024
FFN_FC = V7X_MXU_DIM
Q_SCALE = HEAD_DIM ** -0.5 * math.log2(math.e)


TOK_SUB = 256
N_SUB = TOK_TM // TOK_SUB


def _sub_rows(s):
    return slice(s * TOK_SUB, (s + 1) * TOK_SUB)


def _swiglu_ln(xs, wg_ref, wu_ref, wd_ref, g_ref, b_ref, h_ref):
    xbs = [x.astype(BF16) for x in xs]
    for c in range(D_FF // FFN_FC):
        cols = slice(c * FFN_FC, (c + 1) * FFN_FC)
        for s, xb in enumerate(xbs):
            gate = _dot(xb, wg_ref[:, cols])
            up = _dot(xb, wu_ref[:, cols])
            h_ref[s, :, cols] = (gate * jax.nn.sigmoid(gate) * up).astype(BF16)
    outs = []
    for s, x in enumerate(xs):
        y = _dot(h_ref[s], wd_ref[...])
        outs.append(_layer_norm(ALPHA * x + 0.5 * y, g_ref[...], b_ref[...]))
    return outs


def _ffn_proj_kernel(x_ref, wg_ref, wu_ref, wd_ref, g_ref, b_ref, win_ref,
                     x1_ref, q_ref, k_ref, v_ref, u_ref, h_ref):
    x1s = _swiglu_ln([x_ref[_sub_rows(s), :] for s in range(N_SUB)],
                     wg_ref, wu_ref, wd_ref, g_ref, b_ref, h_ref)
    for s, x1 in enumerate(x1s):
        rows = _sub_rows(s)
        x1_ref[rows, :] = x1
        xb = x1.astype(BF16)
        q_ref[rows, :] = (_dot(xb, win_ref[:, 0:D_ATTN]) * Q_SCALE).astype(BF16)
        k_ref[rows, :] = _dot(xb, win_ref[:, D_ATTN:2 * D_ATTN]).astype(BF16)
        v_ref[rows, :] = _dot(xb, win_ref[:, 2 * D_ATTN:3 * D_ATTN]).astype(BF16)
        u = _dot(xb, win_ref[:, 3 * D_ATTN:])
        for grp in range(N_FOURIER_GROUPS):
            u_ref[grp, rows, :] = u[:, grp * FOURIER_GROUP:(grp + 1) * FOURIER_GROUP]


def _mix_ffn_kernel(attn_ref, four_ref, x_ref, ga_ref, gf_ref, wo_ref, g2_ref, b2_ref,
                    wg_ref, wu_ref, wd_ref, g3_ref, b3_ref, o_ref, h_ref):
    k2_per_step = MIX_TM // SEQ_N1
    four2 = four_ref.reshape(N_FOURIER_GROUPS * SEQ_N1 * k2_per_step, FOURIER_GROUP)

    def four_rows(k2):
        return jnp.concatenate(
            [four2[pl.ds(grp * SEQ_N1 * k2_per_step + k2, SEQ_N1, stride=k2_per_step), :]
             for grp in range(N_FOURIER_GROUPS)], axis=1)

    x2s = []
    for s in range(MIX_TM // TOK_SUB):
        rows = _sub_rows(s)
        ma = _rms_norm(attn_ref[rows, :], ga_ref[...]).astype(BF16)
        four = jnp.concatenate([four_rows(s * (TOK_SUB // SEQ_N1) + t)
                                for t in range(TOK_SUB // SEQ_N1)], axis=0)
        mf = _rms_norm(four, gf_ref[...]).astype(BF16)
        z = _dot(ma, wo_ref[0:D_ATTN, :]) + _dot(mf, wo_ref[D_ATTN:, :])
        x2s.append(_layer_norm(ALPHA * x_ref[rows, :] + z, g2_ref[...], b2_ref[...]))
    outs = _swiglu_ln(x2s, wg_ref, wu_ref, wd_ref, g3_ref, b3_ref, h_ref)
    for s, out in enumerate(outs):
        o_ref[_sub_rows(s), :] = out


def _tok_spec(width, tm=TOK_TM):
    return pl.BlockSpec((tm, width), lambda i: (i, 0))


def _layer_spec(layer, rows, cols):
    return pl.BlockSpec((None, rows, cols), lambda i: (layer, 0, 0), pipeline_mode=pl.Buffered(1))


def _ffn_weight_specs(layer):
    return [_layer_spec(layer, D_MODEL, D_FF), _layer_spec(layer, D_MODEL, D_FF),
            _layer_spec(layer, D_FF, D_MODEL), _const_spec((1, D_MODEL)), _const_spec((1, D_MODEL))]


CAST_STEPS = 16


def _cast_kernel(*refs):
    n = len(refs) // 2
    for src, dst in zip(refs[:n], refs[n:]):
        dst[...] = src[...].astype(BF16)


def _cast_weights(weights):
    flat = [w.reshape(-1, w.shape[-1]) for w in weights]
    specs = [pl.BlockSpec((f.shape[0] // CAST_STEPS, f.shape[1]), lambda i: (i, 0)) for f in flat]
    vmem = 2 * sum(f.size // CAST_STEPS * 6 for f in flat) + (4 << 20)
    outs = pl.pallas_call(
        _cast_kernel,
        grid=(CAST_STEPS,),
        in_specs=specs,
        out_specs=specs,
        out_shape=[jax.ShapeDtypeStruct(f.shape, BF16) for f in flat],
        compiler_params=_cparams(vmem, 1),
        name="cast_weights",
    )(*flat)
    return [o.reshape(w.shape) for o, w in zip(outs, weights)]


def _ffn_proj(layer, x, wg, wu, wd, g, b, w_in):
    n = x.shape[0]
    half = jax.ShapeDtypeStruct((n, D_ATTN), BF16)
    four = jax.ShapeDtypeStruct((N_FOURIER_GROUPS, n, FOURIER_GROUP), F32)
    four_spec = pl.BlockSpec((N_FOURIER_GROUPS, TOK_TM, FOURIER_GROUP), lambda i: (0, i, 0))
    vmem = (3 * D_MODEL * D_FF * 2 + D_MODEL * D_IN * 2 + TOK_TM * D_FF * 2
            + 2 * TOK_TM * (2 * D_MODEL * 4 + 3 * D_ATTN * 2 + D_FOURIER * 4)
            + 8 * TOK_SUB * D_MODEL * 4)
    return pl.pallas_call(
        _ffn_proj_kernel,
        grid=(n // TOK_TM,),
        in_specs=([_tok_spec(D_MODEL)] + _ffn_weight_specs(layer)
                  + [_layer_spec(layer, D_MODEL, D_IN)]),
        out_specs=[_tok_spec(D_MODEL)] + [_tok_spec(D_ATTN)] * 3 + [four_spec],
        out_shape=[jax.ShapeDtypeStruct((n, D_MODEL), F32)] + [half] * 3 + [four],
        scratch_shapes=[pltpu.VMEM((N_SUB, TOK_SUB, D_FF), BF16)],
        compiler_params=_cparams(vmem, 1),
        name="ffn_proj",
    )(x, wg, wu, wd, g, b, w_in)


def _mix_ffn(layer, attn, four, x, ga, gf, wo, g2, b2, wg, wu, wd, g3, b3):
    n = x.shape[0]
    tm = MIX_TM
    k2_per_step = tm // SEQ_N1
    steps_per_batch = SEQ_N2 // k2_per_step
    four_spec = pl.BlockSpec(
        (N_FOURIER_GROUPS, None, SEQ_N1, k2_per_step, FOURIER_GROUP),
        lambda i: (0, i // steps_per_batch, 0, i % steps_per_batch, 0))
    vmem = (3 * D_MODEL * D_FF * 2 + D_MODEL * D_MODEL * 2 + tm * D_FF * 2
            + 2 * tm * (2 * D_MODEL * 4 + D_ATTN * 4 + D_FOURIER * 4)
            + 16 * TOK_SUB * D_MODEL * 4)
    return pl.pallas_call(
        _mix_ffn_kernel,
        grid=(n // tm,),
        in_specs=([_tok_spec(D_ATTN, tm), four_spec, _tok_spec(D_MODEL, tm),
                   _const_spec((1, D_ATTN)), _const_spec((1, D_FOURIER)),
                   _layer_spec(layer, D_MODEL, D_MODEL), _const_spec((1, D_MODEL)),
                   _const_spec((1, D_MODEL))]
                  + _ffn_weight_specs(layer)),
        out_specs=_tok_spec(D_MODEL, tm),
        out_shape=jax.ShapeDtypeStruct((n, D_MODEL), F32),
        scratch_shapes=[pltpu.VMEM((tm // TOK_SUB, TOK_SUB, D_FF), BF16)],
        compiler_params=_cparams(vmem, 1),
        name="mix_ffn",
    )(attn, four, x, ga, gf, wo, g2, b2, wg, wu, wd, g3, b3)


PAIR_Q = 2 * GRID_W
PAIR_KROWS = WIN_ROWS + 2
PAIR_K = PAIR_KROWS * GRID_W
ATT_PAIRS_PER_STEP = 16
ATT_PAIRS_PER_ITER = 4
N_TZ = 2 * WIN_ROWS
LOG2E = math.log2(math.e)
ATT_HALO = (WIN_ROWS // 2) * GRID_W
ATT_KV_WINDOW = ATT_PAIRS_PER_STEP * PAIR_Q + PAIR_K - PAIR_Q


def _kv_window_start(step, seq):
    tiles = jnp.clip(step * (ATT_PAIRS_PER_STEP * PAIR_Q // V7X_LANES) - ATT_HALO // V7X_LANES,
                     0, (seq - ATT_KV_WINDOW) // V7X_LANES)
    return tiles * V7X_LANES


def _toeplitz_tables(rpb_l):
    cols = np.arange(GRID_W)
    c0 = np.clip(cols - WIN_COLS // 2, 0, GRID_W - WIN_COLS)
    in_win = (cols[None, :] >= c0[:, None]) & (cols[None, :] < c0[:, None] + WIN_COLS)
    dc_idx = np.clip(cols[None, :] - cols[:, None], -(WIN_COLS - 1), WIN_COLS - 1) + (WIN_COLS - 1)
    onehot = (dc_idx[None] == np.arange(2 * WIN_COLS - 1)[:, None, None]).astype(np.float32)
    tz = jnp.einsum("hdj,jqk->hdqk", rpb_l.astype(F32) * LOG2E, jnp.asarray(onehot),
                    precision=lax.Precision.HIGHEST)
    tz = jnp.where(in_win, tz, NEG_INF)
    tzp = jnp.pad(tz, ((0, 0), (1, 1), (0, 0), (0, 0)), constant_values=NEG_INF)
    left, right = tzp[:, :-1], tzp[:, 1:]
    masked = jnp.full_like(left, NEG_INF)
    return jnp.concatenate([jnp.concatenate([left, right], axis=-1),
                            jnp.concatenate([left, masked], axis=-1),
                            jnp.concatenate([masked, right], axis=-1)], axis=1)


def _nattn_kernel(q_ref, k_ref, v_ref, tz_ref, o_ref, s_ref, *, n_rows):
    step = pl.program_id(1)
    kv_start = _kv_window_start(step, n_rows * GRID_W)
    lane = lax.broadcasted_iota(jnp.int32, (PAIR_Q, V7X_LANES), 1)
    first_head = lane < HEAD_DIM
    ones_block = jnp.ones((PAIR_K, V7X_LANES), BF16)

    def pair_geometry(pl_idx):
        p = step * ATT_PAIRS_PER_STEP + pl_idx
        w = jnp.clip(2 * p - WIN_ROWS // 2, 0, n_rows - PAIR_KROWS)

        def tz_index(a, t):
            r = 2 * p + a
            r0 = jnp.clip(r - WIN_ROWS // 2, 0, n_rows - WIN_ROWS)
            kr = w + 2 * t
            ok0 = (kr >= r0) & (kr < r0 + WIN_ROWS)
            ok1 = (kr + 1 >= r0) & (kr + 1 < r0 + WIN_ROWS)
            i = jnp.clip(kr - r + WIN_ROWS, 0, N_TZ - 1)
            return jnp.where(ok0 & ok1, i,
                             jnp.where(ok0, N_TZ + i, jnp.where(ok1, 2 * N_TZ + i, N_TZ)))

        return dict(
            qrows=pl.ds(pl.multiple_of(pl_idx * PAIR_Q, PAIR_Q), PAIR_Q),
            krows=pl.ds(pl.multiple_of(w * GRID_W - kv_start, V7X_LANES), PAIR_K),
            tz_idx=[[tz_index(a, t) for t in range(PAIR_KROWS // 2)] for a in (0, 1)])

    def scores(geo, hp):
        lanes = slice(hp * V7X_LANES, (hp + 1) * V7X_LANES)
        qp = q_ref[geo["qrows"], lanes]
        kp = k_ref[geo["krows"], lanes]
        zero = jnp.zeros_like(qp)
        qq = jnp.concatenate([jnp.where(first_head, qp, zero),
                              jnp.where(first_head, zero, qp)], axis=0)
        return lax.dot_general(qq, kp, (((1,), (1,)), ((), ())),
                               preferred_element_type=F32)

    def finish(geo, hp, s):
        lanes = slice(hp * V7X_LANES, (hp + 1) * V7X_LANES)
        probs = []
        for e in (0, 1):
            h = 2 * hp + e
            bias = jnp.concatenate(
                [jnp.concatenate([tz_ref[h, geo["tz_idx"][a][t]] for t in range(PAIR_KROWS // 2)],
                                 axis=1) for a in (0, 1)], axis=0)
            se = s[e * PAIR_Q:(e + 1) * PAIR_Q] + bias
            m = jnp.max(se, axis=-1, keepdims=True)
            probs.append(jnp.exp2(se - m).astype(BF16))
        v_ext = jnp.concatenate([v_ref[geo["krows"], lanes], ones_block], axis=1)
        o = _dot(jnp.concatenate(probs, axis=0), v_ext)
        o = o[:, :V7X_LANES] / o[:, V7X_LANES:]
        o_ref[geo["qrows"], lanes] = jnp.where(first_head, o[:PAIR_Q], o[PAIR_Q:])

    def body(it, carry):
        geos = [pair_geometry(it * ATT_PAIRS_PER_ITER + j) for j in range(ATT_PAIRS_PER_ITER)]
        units = [(geos[j], hp) for j in range(ATT_PAIRS_PER_ITER)
                 for hp in range(N_ATTN_HEADS // 2)]
        s_cur = s_ref[...]
        for u in range(1, len(units)):
            s_next = scores(*units[u])
            finish(*units[u - 1], s_cur)
            s_cur = s_next
        nxt = jnp.minimum((it + 1) * ATT_PAIRS_PER_ITER, ATT_PAIRS_PER_STEP - 1)
        s_ref[...] = scores(pair_geometry(nxt), 0)
        finish(*units[-1], s_cur)
        return carry

    s_ref[...] = scores(pair_geometry(0), 0)
    lax.fori_loop(0, ATT_PAIRS_PER_STEP // ATT_PAIRS_PER_ITER, body, 0)


def _nattn(q, k, v, tz2, batch, seq):
    n_rows = seq // GRID_W
    tq = ATT_PAIRS_PER_STEP * PAIR_Q
    steps = seq // tq
    k3 = k.reshape(batch, seq, D_ATTN)
    v3 = v.reshape(batch, seq, D_ATTN)
    vmem = (2 * 2 * ATT_KV_WINDOW * D_ATTN * 2
            + tz2.size * 4
            + 2 * tq * D_ATTN * (2 + 4)
            + (8 << 20))
    kv_spec = pl.BlockSpec(
        (None, pl.Element(ATT_KV_WINDOW), pl.Element(D_ATTN)),
        lambda b, i: (b, _kv_window_start(i, seq), 0))
    return pl.pallas_call(
        functools.partial(_nattn_kernel, n_rows=n_rows),
        grid=(batch, steps),
        in_specs=[
            pl.BlockSpec((tq, D_ATTN), lambda b, i: (b * steps + i, 0)),
            kv_spec,
            kv_spec,
            _const_spec(tz2.shape),
        ],
        out_specs=pl.BlockSpec((tq, D_ATTN), lambda b, i: (b * steps + i, 0)),
        out_shape=jax.ShapeDtypeStruct((batch * seq, D_ATTN), F32),
        scratch_shapes=[pltpu.VMEM((2 * PAIR_Q, PAIR_K), F32)],
        compiler_params=_cparams(vmem, 2),
        name="nattn",
    )(q, k3, v3, tz2)


SEQ_N1 = 128
SEQ_N2 = 64
FN1_S2_PER_STEP = 8
FN2_K1_PER_STEP = 32
FN1_ROWS = SEQ_N1 * FN1_S2_PER_STEP


FN1_STEPS = SEQ_N2 // FN1_S2_PER_STEP
P_SLAB_PITCH = SEQ_N1 + 8
FN2_STEPS = SEQ_N1 // FN2_K1_PER_STEP


def _fnet_kernel(u_ref, cs_ref, w1_ref, tc_ref, ts_ref, w2_ref, y_ref, pr_ref, pi_ref):
    j = pl.program_id(1)

    @pl.when(j < FN1_STEPS)
    def _stage1():
        u2 = u_ref.reshape(N_FOURIER_GROUPS * FN1_ROWS, FOURIER_GROUP)
        tcos = tc_ref[...]
        tsin = ts_ref[...]

        def rows(grp, jj):
            return pl.ds(grp * FN1_ROWS + jj, SEQ_N1, stride=FN1_S2_PER_STEP)

        for jj in range(FN1_S2_PER_STEP):
            ab = [_dot(u2[rows(grp, jj), :].astype(BF16), cs_ref[...])
                  for grp in range(N_FOURIER_GROUPS)]
            x = jnp.concatenate(
                [jnp.concatenate([t[:, :FOURIER_GROUP] for t in ab], axis=1),
                 jnp.concatenate([t[:, FOURIER_GROUP:] for t in ab], axis=1)],
                axis=0).astype(BF16)
            t = _dot(w1_ref[...], x)
            tr = t[:SEQ_N1]
            ti = t[SEQ_N1:]
            c = tcos[:, jj:jj + 1]
            s = tsin[:, jj:jj + 1]
            pr = tr * c + ti * s
            pi = ti * c - tr * s
            for grp in range(N_FOURIER_GROUPS):
                cols = slice(grp * FOURIER_GROUP, (grp + 1) * FOURIER_GROUP)
                slab0 = (grp * FN1_S2_PER_STEP + jj) * P_SLAB_PITCH
                pr_ref[j, pl.ds(slab0, SEQ_N1), :] = pr[:, cols]
                pi_ref[j, pl.ds(slab0, SEQ_N1), :] = pi[:, cols]

    @pl.when(j >= FN1_STEPS)
    def _stage2():
        kb = j - FN1_STEPS

        def slab(ref, kk):
            def tile(grp, blk):
                row0 = grp * FN1_S2_PER_STEP * P_SLAB_PITCH + kb * FN2_K1_PER_STEP + kk
                return ref[blk, pl.ds(row0, FN1_S2_PER_STEP, stride=P_SLAB_PITCH), :]
            return jnp.concatenate(
                [jnp.concatenate([tile(grp, blk) for blk in range(FN1_STEPS)], axis=0)
                 for grp in range(N_FOURIER_GROUPS)], axis=1).astype(BF16)

        for kk in range(FN2_K1_PER_STEP):
            x = jnp.concatenate([slab(pr_ref, kk), slab(pi_ref, kk)], axis=0)
            res = _dot(w2_ref[...], x)
            for grp in range(N_FOURIER_GROUPS):
                y_ref[grp, kk] = res[:, grp * FOURIER_GROUP:(grp + 1) * FOURIER_GROUP]


def _fnet(u, chan, w1, w2, tcos, tsin, batch, seq):
    shape5 = (N_FOURIER_GROUPS, batch, SEQ_N1, SEQ_N2, FOURIER_GROUP)
    u5 = u.reshape(shape5)
    last1 = FN1_STEPS - 1
    in_spec = pl.BlockSpec((N_FOURIER_GROUPS, None, SEQ_N1, FN1_S2_PER_STEP, FOURIER_GROUP),
                           lambda bb, j: (0, bb, 0, jnp.minimum(j, last1), 0))
    tw_spec = pl.BlockSpec((None, SEQ_N1, FN1_S2_PER_STEP),
                           lambda bb, j: (jnp.minimum(j, last1), 0, 0))
    out_spec = pl.BlockSpec(
        (N_FOURIER_GROUPS, None, FN2_K1_PER_STEP, SEQ_N2, FOURIER_GROUP),
        lambda bb, j: (0, bb, jnp.maximum(j - FN1_STEPS, 0), 0, 0))
    scratch_rows = N_FOURIER_GROUPS * FN1_S2_PER_STEP * P_SLAB_PITCH
    scratch = pltpu.VMEM((FN1_STEPS, scratch_rows, FOURIER_GROUP), F32)
    blk_bytes = N_FOURIER_GROUPS * FN1_ROWS * FOURIER_GROUP * 4
    vmem = (2 * FN1_STEPS * scratch_rows * FOURIER_GROUP * 4
            + 2 * blk_bytes
            + 2 * SEQ_N2 * FN2_K1_PER_STEP * D_FOURIER * 4
            + (8 << 20))
    return pl.pallas_call(
        _fnet_kernel,
        grid=(batch, FN1_STEPS + FN2_STEPS),
        in_specs=[in_spec, _const_spec((FOURIER_GROUP, 2 * FOURIER_GROUP)),
                  _const_spec((2 * SEQ_N1, 2 * SEQ_N1)), tw_spec, tw_spec,
                  _const_spec((SEQ_N2, 2 * SEQ_N2))],
        out_specs=out_spec,
        out_shape=jax.ShapeDtypeStruct(
            (N_FOURIER_GROUPS, batch, SEQ_N1, SEQ_N2, FOURIER_GROUP), F32),
        scratch_shapes=[scratch, scratch],
        compiler_params=_cparams(vmem, 2),
        name="fnet",
    )(u5, chan, w1, tcos, tsin, w2)


def _dft_constants():
    def cs(n):
        idx = np.arange(n)
        ang = 2.0 * np.pi * ((idx[:, None] * idx[None, :]) % n) / n
        return np.cos(ang), np.sin(ang)

    cc, sc = cs(FOURIER_GROUP)
    chan = np.concatenate([cc, sc], axis=1)
    c1, s1 = cs(SEQ_N1)
    w1 = np.block([[c1, -s1], [-s1, -c1]])
    c2, s2 = cs(SEQ_N2)
    norm = 1.0 / math.sqrt(SEQ_N1 * SEQ_N2 * FOURIER_GROUP)
    w2 = np.concatenate([c2, s2], axis=1) * norm
    k1 = np.arange(SEQ_N1)
    s2i = np.arange(SEQ_N2)
    ang = 2.0 * np.pi * (k1[:, None] * s2i[None, :]) / (SEQ_N1 * SEQ_N2)
    nblk = SEQ_N2 // FN1_S2_PER_STEP

    def blocked(t):
        return t.reshape(SEQ_N1, nblk, FN1_S2_PER_STEP).transpose(1, 0, 2)

    return (jnp.asarray(chan, F32).astype(BF16), jnp.asarray(w1, F32).astype(BF16),
            jnp.asarray(w2, F32).astype(BF16),
            jnp.asarray(blocked(np.cos(ang)), F32), jnp.asarray(blocked(np.sin(ang)), F32))


def kernel(x, ffn1_w_gate, ffn1_w_up, ffn1_w_down, ln1_g, ln1_b, w_in, rpb, g_attn, g_fourier,
           w_out, ln2_g, ln2_b, ffn2_w_gate, ffn2_w_up, ffn2_w_down, ln3_g, ln3_b):
    batch, seq, d = x.shape
    assert d == D_MODEL and seq == SEQ_N1 * SEQ_N2 and seq % (ATT_PAIRS_PER_STEP * PAIR_Q) == 0
    assert rpb.shape[0] == DEPTH
    chan, w1, w2, tcos, tsin = _dft_constants()

    def row(p, l):
        return p[l].reshape(1, -1).astype(F32)

    wg1, wu1, wd1, win, wout, wg2, wu2, wd2 = _cast_weights(
        [ffn1_w_gate, ffn1_w_up, ffn1_w_down, w_in, w_out, ffn2_w_gate, ffn2_w_up, ffn2_w_down])

    h = x.reshape(batch * seq, d).astype(F32)
    for l in range(DEPTH):
        h, q, k, v, u = _ffn_proj(l, h, wg1, wu1, wd1, row(ln1_g, l), row(ln1_b, l), win)
        attn = _nattn(q, k, v, _toeplitz_tables(rpb[l]), batch, seq)
        four = _fnet(u, chan, w1, w2, tcos, tsin, batch, seq)
        h = _mix_ffn(l, attn, four, h, row(g_attn, l), row(g_fourier, l), wout,
                     row(ln2_g, l), row(ln2_b, l), wg2, wu2, wd2, row(ln3_g, l), row(ln3_b, l))
    return h.reshape(batch, seq, d).astype(x.dtype)
```

```python
import functools
import math

import jax
import jax.numpy as jnp
import numpy as np
from jax import lax
from jax.experimental import pallas as pl
from jax.experimental.pallas import tpu as pltpu

D_MODEL = 1024
DEPTH = 2
GRID_W = 64
WIN_ROWS = 8
WIN_COLS = 16
N_ATTN_HEADS = 8
HEAD_DIM = 64
D_ATTN = N_ATTN_HEADS * HEAD_DIM
N_FOURIER_GROUPS = 4
FOURIER_GROUP = 128
D_FOURIER = N_FOURIER_GROUPS * FOURIER_GROUP
D_IN = 3 * D_ATTN + D_FOURIER
D_FF = 2816
ALPHA = (2.0 * DEPTH) ** 0.25
LN_EPS = 1e-5
RMS_EPS = 1e-6
NEG_INF = -1e30

V7X_LANES = 128
V7X_MXU_DIM = 256
V7X_VMEM_BYTES = 64 * 1024 * 1024

F32 = jnp.float32
BF16 = jnp.bfloat16


def _cparams(vmem_bytes, n_axes):
    return pltpu.CompilerParams(
        dimension_semantics=("arbitrary",) * n_axes,
        vmem_limit_bytes=min(int(vmem_bytes), V7X_VMEM_BYTES - (2 << 20)),
    )


def _const_spec(shape):
    nd = len(shape)
    return pl.BlockSpec(shape, lambda *_: (0,) * nd, pipeline_mode=pl.Buffered(1))


def _layer_norm(z, g, b):
    mu = jnp.mean(z, axis=-1, keepdims=True)
    zc = z - mu
    var = jnp.mean(zc * zc, axis=-1, keepdims=True)
    return zc * lax.rsqrt(var + LN_EPS) * g + b


def _rms_norm(z, g):
    return z * lax.rsqrt(jnp.mean(z * z, axis=-1, keepdims=True) + RMS_EPS) * g


def _dot(a, b):
    return jnp.dot(a, b, preferred_element_type=F32)


TOK_TM = 1024
MIX_TM = 1024
FFN_FC = V7X_MXU_DIM
Q_SCALE = HEAD_DIM ** -0.5 * math.log2(math.e)


TOK_SUB = 256
N_SUB = TOK_TM // TOK_SUB


def _sub_rows(s):
    return slice(s * TOK_SUB, (s + 1) * TOK_SUB)


def _swiglu_ln(xs, wg_ref, wu_ref, wd_ref, g_ref, b_ref, h_ref):
    xbs = [x.astype(BF16) for x in xs]
    for c in range(D_FF // FFN_FC):
        cols = slice(c * FFN_FC, (c + 1) * FFN_FC)
        for s, xb in enumerate(xbs):
            gate = _dot(xb, wg_ref[:, cols])
            up = _dot(xb, wu_ref[:, cols])
            h_ref[s, :, cols] = (gate * jax.nn.sigmoid(gate) * up).astype(BF16)
    outs = []
    for s, x in enumerate(xs):
        y = _dot(h_ref[s], wd_ref[...])
        outs.append(_layer_norm(ALPHA * x + 0.5 * y, g_ref[...], b_ref[...]))
    return outs


def _ffn_proj_kernel(x_ref, wg_ref, wu_ref, wd_ref, g_ref, b_ref, win_ref,
                     x1_ref, q_ref, k_ref, v_ref, u_ref, h_ref):
    x1s = _swiglu_ln([x_ref[_sub_rows(s), :] for s in range(N_SUB)],
                     wg_ref, wu_ref, wd_ref, g_ref, b_ref, h_ref)
    for s, x1 in enumerate(x1s):
        rows = _sub_rows(s)
        x1_ref[rows, :] = x1
        xb = x1.astype(BF16)
        q_ref[rows, :] = (_dot(xb, win_ref[:, 0:D_ATTN]) * Q_SCALE).astype(BF16)
        k_ref[rows, :] = _dot(xb, win_ref[:, D_ATTN:2 * D_ATTN]).astype(BF16)
        v_ref[rows, :] = _dot(xb, win_ref[:, 2 * D_ATTN:3 * D_ATTN]).astype(BF16)
        u = _dot(xb, win_ref[:, 3 * D_ATTN:])
        for grp in range(N_FOURIER_GROUPS):
            u_ref[grp, rows, :] = u[:, grp * FOURIER_GROUP:(grp + 1) * FOURIER_GROUP]


def _mix_ffn_kernel(attn_ref, four_ref, x_ref, ga_ref, gf_ref, wo_ref, g2_ref, b2_ref,
                    wg_ref, wu_ref, wd_ref, g3_ref, b3_ref, o_ref, h_ref):
    x2s = []
    for s in range(MIX_TM // TOK_SUB):
        rows = _sub_rows(s)
        ma = _rms_norm(attn_ref[rows, :], ga_ref[...]).astype(BF16)
        mf = _rms_norm(four_ref[rows, :], gf_ref[...]).astype(BF16)
        z = _dot(ma, wo_ref[0:D_ATTN, :]) + _dot(mf, wo_ref[D_ATTN:, :])
        x2s.append(_layer_norm(ALPHA * x_ref[rows, :] + z, g2_ref[...], b2_ref[...]))
    outs = _swiglu_ln(x2s, wg_ref, wu_ref, wd_ref, g3_ref, b3_ref, h_ref)
    for s, out in enumerate(outs):
        o_ref[_sub_rows(s), :] = out


def _tok_spec(width, tm=TOK_TM):
    return pl.BlockSpec((tm, width), lambda i: (i, 0))


def _layer_spec(layer, rows, cols):
    return pl.BlockSpec((None, rows, cols), lambda i: (layer, 0, 0), pipeline_mode=pl.Buffered(1))


def _ffn_weight_specs(layer):
    return [_layer_spec(layer, D_MODEL, D_FF), _layer_spec(layer, D_MODEL, D_FF),
            _layer_spec(layer, D_FF, D_MODEL), _const_spec((1, D_MODEL)), _const_spec((1, D_MODEL))]


CAST_STEPS = 16


def _cast_kernel(*refs):
    n = len(refs) // 2
    for src, dst in zip(refs[:n], refs[n:]):
        dst[...] = src[...].astype(BF16)


def _cast_weights(weights):
    flat = [w.reshape(-1, w.shape[-1]) for w in weights]
    specs = [pl.BlockSpec((f.shape[0] // CAST_STEPS, f.shape[1]), lambda i: (i, 0)) for f in flat]
    vmem = 2 * sum(f.size // CAST_STEPS * 6 for f in flat) + (4 << 20)
    outs = pl.pallas_call(
        _cast_kernel,
        grid=(CAST_STEPS,),
        in_specs=specs,
        out_specs=specs,
        out_shape=[jax.ShapeDtypeStruct(f.shape, BF16) for f in flat],
        compiler_params=_cparams(vmem, 1),
        name="cast_weights",
    )(*flat)
    return [o.reshape(w.shape) for o, w in zip(outs, weights)]


def _ffn_proj(layer, x, wg, wu, wd, g, b, w_in):
    n = x.shape[0]
    half = jax.ShapeDtypeStruct((n, D_ATTN), BF16)
    four = jax.ShapeDtypeStruct((N_FOURIER_GROUPS, n, FOURIER_GROUP), F32)
    four_spec = pl.BlockSpec((N_FOURIER_GROUPS, TOK_TM, FOURIER_GROUP), lambda i: (0, i, 0))
    vmem = (3 * D_MODEL * D_FF * 2 + D_MODEL * D_IN * 2 + TOK_TM * D_FF * 2
            + 2 * TOK_TM * (2 * D_MODEL * 4 + 3 * D_ATTN * 2 + D_FOURIER * 4)
            + 8 * TOK_SUB * D_MODEL * 4)
    return pl.pallas_call(
        _ffn_proj_kernel,
        grid=(n // TOK_TM,),
        in_specs=([_tok_spec(D_MODEL)] + _ffn_weight_specs(layer)
                  + [_layer_spec(layer, D_MODEL, D_IN)]),
        out_specs=[_tok_spec(D_MODEL)] + [_tok_spec(D_ATTN)] * 3 + [four_spec],
        out_shape=[jax.ShapeDtypeStruct((n, D_MODEL), F32)] + [half] * 3 + [four],
        scratch_shapes=[pltpu.VMEM((N_SUB, TOK_SUB, D_FF), BF16)],
        compiler_params=_cparams(vmem, 1),
        name="ffn_proj",
    )(x, wg, wu, wd, g, b, w_in)


def _mix_ffn(layer, attn, four, x, ga, gf, wo, g2, b2, wg, wu, wd, g3, b3):
    n = x.shape[0]
    tm = MIX_TM
    vmem = (3 * D_MODEL * D_FF * 2 + D_MODEL * D_MODEL * 2 + tm * D_FF * 2
            + 2 * tm * (2 * D_MODEL * 4 + D_ATTN * 4 + D_FOURIER * 4)
            + 16 * TOK_SUB * D_MODEL * 4)
    return pl.pallas_call(
        _mix_ffn_kernel,
        grid=(n // tm,),
        in_specs=([_tok_spec(D_ATTN, tm), _tok_spec(D_FOURIER, tm), _tok_spec(D_MODEL, tm),
                   _const_spec((1, D_ATTN)), _const_spec((1, D_FOURIER)),
                   _layer_spec(layer, D_MODEL, D_MODEL), _const_spec((1, D_MODEL)),
                   _const_spec((1, D_MODEL))]
                  + _ffn_weight_specs(layer)),
        out_specs=_tok_spec(D_MODEL, tm),
        out_shape=jax.ShapeDtypeStruct((n, D_MODEL), F32),
        scratch_shapes=[pltpu.VMEM((tm // TOK_SUB, TOK_SUB, D_FF), BF16)],
        compiler_params=_cparams(vmem, 1),
        name="mix_ffn",
    )(attn, four, x, ga, gf, wo, g2, b2, wg, wu, wd, g3, b3)


PAIR_Q = 2 * GRID_W
PAIR_KROWS = WIN_ROWS + 2
PAIR_K = PAIR_KROWS * GRID_W
ATT_PAIRS_PER_STEP = 16
ATT_PAIRS_PER_ITER = 4
N_TZ = 2 * WIN_ROWS
LOG2E = math.log2(math.e)
ATT_HALO = (WIN_ROWS // 2) * GRID_W
ATT_KV_WINDOW = ATT_PAIRS_PER_STEP * PAIR_Q + PAIR_K - PAIR_Q


def _kv_window_start(step, seq):
    tiles = jnp.clip(step * (ATT_PAIRS_PER_STEP * PAIR_Q // V7X_LANES) - ATT_HALO // V7X_LANES,
                     0, (seq - ATT_KV_WINDOW) // V7X_LANES)
    return tiles * V7X_LANES


def _toeplitz_tables(rpb_l):
    cols = np.arange(GRID_W)
    c0 = np.clip(cols - WIN_COLS // 2, 0, GRID_W - WIN_COLS)
    in_win = (cols[None, :] >= c0[:, None]) & (cols[None, :] < c0[:, None] + WIN_COLS)
    dc_idx = np.clip(cols[None, :] - cols[:, None], -(WIN_COLS - 1), WIN_COLS - 1) + (WIN_COLS - 1)
    onehot = (dc_idx[None] == np.arange(2 * WIN_COLS - 1)[:, None, None]).astype(np.float32)
    tz = jnp.einsum("hdj,jqk->hdqk", rpb_l.astype(F32) * LOG2E, jnp.asarray(onehot),
                    precision=lax.Precision.HIGHEST)
    tz = jnp.where(in_win, tz, NEG_INF)
    tzp = jnp.pad(tz, ((0, 0), (1, 1), (0, 0), (0, 0)), constant_values=NEG_INF)
    left, right = tzp[:, :-1], tzp[:, 1:]
    masked = jnp.full_like(left, NEG_INF)
    return jnp.concatenate([jnp.concatenate([left, right], axis=-1),
                            jnp.concatenate([left, masked], axis=-1),
                            jnp.concatenate([masked, right], axis=-1)], axis=1)


def _nattn_kernel(q_ref, k_ref, v_ref, tz_ref, o_ref, s_ref, *, n_rows):
    step = pl.program_id(1)
    kv_start = _kv_window_start(step, n_rows * GRID_W)
    lane = lax.broadcasted_iota(jnp.int32, (PAIR_Q, V7X_LANES), 1)
    first_head = lane < HEAD_DIM
    ones_block = jnp.ones((PAIR_K, V7X_LANES), BF16)

    def pair_geometry(pl_idx):
        p = step * ATT_PAIRS_PER_STEP + pl_idx
        w = jnp.clip(2 * p - WIN_ROWS // 2, 0, n_rows - PAIR_KROWS)

        def tz_index(a, t):
            r = 2 * p + a
            r0 = jnp.clip(r - WIN_ROWS // 2, 0, n_rows - WIN_ROWS)
            kr = w + 2 * t
            ok0 = (kr >= r0) & (kr < r0 + WIN_ROWS)
            ok1 = (kr + 1 >= r0) & (kr + 1 < r0 + WIN_ROWS)
            i = jnp.clip(kr - r + WIN_ROWS, 0, N_TZ - 1)
            return jnp.where(ok0 & ok1, i,
                             jnp.where(ok0, N_TZ + i, jnp.where(ok1, 2 * N_TZ + i, N_TZ)))

        return dict(
            qrows=pl.ds(pl.multiple_of(pl_idx * PAIR_Q, PAIR_Q), PAIR_Q),
            krows=pl.ds(pl.multiple_of(w * GRID_W - kv_start, V7X_LANES), PAIR_K),
            tz_idx=[[tz_index(a, t) for t in range(PAIR_KROWS // 2)] for a in (0, 1)])

    def scores(geo, hp):
        lanes = slice(hp * V7X_LANES, (hp + 1) * V7X_LANES)
        qp = q_ref[geo["qrows"], lanes]
        kp = k_ref[geo["krows"], lanes]
        zero = jnp.zeros_like(qp)
        qq = jnp.concatenate([jnp.where(first_head, qp, zero),
                              jnp.where(first_head, zero, qp)], axis=0)
        return lax.dot_general(qq, kp, (((1,), (1,)), ((), ())),
                               preferred_element_type=F32)

    def finish(geo, hp, s):
        lanes = slice(hp * V7X_LANES, (hp + 1) * V7X_LANES)
        probs = []
        for e in (0, 1):
            h = 2 * hp + e
            bias = jnp.concatenate(
                [jnp.concatenate([tz_ref[h, geo["tz_idx"][a][t]] for t in range(PAIR_KROWS // 2)],
                                 axis=1) for a in (0, 1)], axis=0)
            se = s[e * PAIR_Q:(e + 1) * PAIR_Q] + bias
            m = jnp.max(se, axis=-1, keepdims=True)
            probs.append(jnp.exp2(se - m).astype(BF16))
        v_ext = jnp.concatenate([v_ref[geo["krows"], lanes], ones_block], axis=1)
        o = _dot(jnp.concatenate(probs, axis=0), v_ext)
        o = o[:, :V7X_LANES] / o[:, V7X_LANES:]
        o_ref[geo["qrows"], lanes] = jnp.where(first_head, o[:PAIR_Q], o[PAIR_Q:])

    def body(it, carry):
        geos = [pair_geometry(it * ATT_PAIRS_PER_ITER + j) for j in range(ATT_PAIRS_PER_ITER)]
        units = [(geos[j], hp) for j in range(ATT_PAIRS_PER_ITER)
                 for hp in range(N_ATTN_HEADS // 2)]
        s_cur = s_ref[...]
        for u in range(1, len(units)):
            s_next = scores(*units[u])
            finish(*units[u - 1], s_cur)
            s_cur = s_next
        nxt = jnp.minimum((it + 1) * ATT_PAIRS_PER_ITER, ATT_PAIRS_PER_STEP - 1)
        s_ref[...] = scores(pair_geometry(nxt), 0)
        finish(*units[-1], s_cur)
        return carry

    s_ref[...] = scores(pair_geometry(0), 0)
    lax.fori_loop(0, ATT_PAIRS_PER_STEP // ATT_PAIRS_PER_ITER, body, 0)


def _nattn(q, k, v, tz2, batch, seq):
    n_rows = seq // GRID_W
    tq = ATT_PAIRS_PER_STEP * PAIR_Q
    steps = seq // tq
    k3 = k.reshape(batch, seq, D_ATTN)
    v3 = v.reshape(batch, seq, D_ATTN)
    vmem = (2 * 2 * ATT_KV_WINDOW * D_ATTN * 2
            + tz2.size * 4
            + 2 * tq * D_ATTN * (2 + 4)
            + (8 << 20))
    kv_spec = pl.BlockSpec(
        (None, pl.Element(ATT_KV_WINDOW), pl.Element(D_ATTN)),
        lambda b, i: (b, _kv_window_start(i, seq), 0))
    return pl.pallas_call(
        functools.partial(_nattn_kernel, n_rows=n_rows),
        grid=(batch, steps),
        in_specs=[
            pl.BlockSpec((tq, D_ATTN), lambda b, i: (b * steps + i, 0)),
            kv_spec,
            kv_spec,
            _const_spec(tz2.shape),
        ],
        out_specs=pl.BlockSpec((tq, D_ATTN), lambda b, i: (b * steps + i, 0)),
        out_shape=jax.ShapeDtypeStruct((batch * seq, D_ATTN), F32),
        scratch_shapes=[pltpu.VMEM((2 * PAIR_Q, PAIR_K), F32)],
        compiler_params=_cparams(vmem, 2),
        name="nattn",
    )(q, k3, v3, tz2)


SEQ_N1 = 128
SEQ_N2 = 64
FN1_S2_PER_STEP = 8
FN2_K1_PER_STEP = 16
FN1_ROWS = SEQ_N1 * FN1_S2_PER_STEP


FN1_STEPS = SEQ_N2 // FN1_S2_PER_STEP
P_SLAB_PITCH = SEQ_N1 + 8
FN2_STEPS = SEQ_N1 // FN2_K1_PER_STEP


def _fnet_kernel(u_ref, cs_ref, w1_ref, tc_ref, ts_ref, w2_ref, y_ref, pr_ref, pi_ref):
    j = pl.program_id(1)

    @pl.when(j < FN1_STEPS)
    def _stage1():
        u2 = u_ref.reshape(N_FOURIER_GROUPS * FN1_ROWS, FOURIER_GROUP)
        tcos = tc_ref[...]
        tsin = ts_ref[...]

        def rows(grp, jj):
            return pl.ds(grp * FN1_ROWS + jj, SEQ_N1, stride=FN1_S2_PER_STEP)

        for jj in range(FN1_S2_PER_STEP):
            ab = [_dot(u2[rows(grp, jj), :].astype(BF16), cs_ref[...])
                  for grp in range(N_FOURIER_GROUPS)]
            x = jnp.concatenate(
                [jnp.concatenate([t[:, :FOURIER_GROUP] for t in ab], axis=1),
                 jnp.concatenate([t[:, FOURIER_GROUP:] for t in ab], axis=1)],
                axis=0).astype(BF16)
            t = _dot(w1_ref[...], x)
            tr = t[:SEQ_N1]
            ti = t[SEQ_N1:]
            c = tcos[:, jj:jj + 1]
            s = tsin[:, jj:jj + 1]
            pr = tr * c + ti * s
            pi = ti * c - tr * s
            for grp in range(N_FOURIER_GROUPS):
                cols = slice(grp * FOURIER_GROUP, (grp + 1) * FOURIER_GROUP)
                slab0 = (grp * FN1_S2_PER_STEP + jj) * P_SLAB_PITCH
                pr_ref[j, pl.ds(slab0, SEQ_N1), :] = pr[:, cols]
                pi_ref[j, pl.ds(slab0, SEQ_N1), :] = pi[:, cols]

    @pl.when(j >= FN1_STEPS)
    def _stage2():
        kb = j - FN1_STEPS

        def slab(ref, kk):
            def tile(grp, blk):
                row0 = grp * FN1_S2_PER_STEP * P_SLAB_PITCH + kb * FN2_K1_PER_STEP + kk
                return ref[blk, pl.ds(row0, FN1_S2_PER_STEP, stride=P_SLAB_PITCH), :]
            return jnp.concatenate(
                [jnp.concatenate([tile(grp, blk) for blk in range(FN1_STEPS)], axis=0)
                 for grp in range(N_FOURIER_GROUPS)], axis=1).astype(BF16)

        for kk in range(FN2_K1_PER_STEP):
            x = jnp.concatenate([slab(pr_ref, kk), slab(pi_ref, kk)], axis=0)
            y_ref[:, kk, :] = _dot(w2_ref[...], x)


def _fnet(u, chan, w1, w2, tcos, tsin, batch, seq):
    shape5 = (N_FOURIER_GROUPS, batch, SEQ_N1, SEQ_N2, FOURIER_GROUP)
    u5 = u.reshape(shape5)
    last1 = FN1_STEPS - 1
    in_spec = pl.BlockSpec((N_FOURIER_GROUPS, None, SEQ_N1, FN1_S2_PER_STEP, FOURIER_GROUP),
                           lambda bb, j: (0, bb, 0, jnp.minimum(j, last1), 0))
    tw_spec = pl.BlockSpec((None, SEQ_N1, FN1_S2_PER_STEP),
                           lambda bb, j: (jnp.minimum(j, last1), 0, 0))
    out_spec = pl.BlockSpec((None, SEQ_N2, FN2_K1_PER_STEP, D_FOURIER),
                            lambda bb, j: (bb, 0, jnp.maximum(j - FN1_STEPS, 0), 0))
    scratch_rows = N_FOURIER_GROUPS * FN1_S2_PER_STEP * P_SLAB_PITCH
    scratch = pltpu.VMEM((FN1_STEPS, scratch_rows, FOURIER_GROUP), F32)
    blk_bytes = N_FOURIER_GROUPS * FN1_ROWS * FOURIER_GROUP * 4
    vmem = (2 * FN1_STEPS * scratch_rows * FOURIER_GROUP * 4
            + 2 * blk_bytes
            + 2 * SEQ_N2 * FN2_K1_PER_STEP * D_FOURIER * 4
            + (8 << 20))
    y = pl.pallas_call(
        _fnet_kernel,
        grid=(batch, FN1_STEPS + FN2_STEPS),
        in_specs=[in_spec, _const_spec((FOURIER_GROUP, 2 * FOURIER_GROUP)),
                  _const_spec((2 * SEQ_N1, 2 * SEQ_N1)), tw_spec, tw_spec,
                  _const_spec((SEQ_N2, 2 * SEQ_N2))],
        out_specs=out_spec,
        out_shape=jax.ShapeDtypeStruct((batch, SEQ_N2, SEQ_N1, D_FOURIER), F32),
        scratch_shapes=[scratch, scratch],
        compiler_params=_cparams(vmem, 2),
        name="fnet",
    )(u5, chan, w1, tcos, tsin, w2)
    return y.reshape(batch * seq, D_FOURIER)


def _dft_constants():
    def cs(n):
        idx = np.arange(n)
        ang = 2.0 * np.pi * ((idx[:, None] * idx[None, :]) % n) / n
        return np.cos(ang), np.sin(ang)

    cc, sc = cs(FOURIER_GROUP)
    chan = np.concatenate([cc, sc], axis=1)
    c1, s1 = cs(SEQ_N1)
    w1 = np.block([[c1, -s1], [-s1, -c1]])
    c2, s2 = cs(SEQ_N2)
    norm = 1.0 / math.sqrt(SEQ_N1 * SEQ_N2 * FOURIER_GROUP)
    w2 = np.concatenate([c2, s2], axis=1) * norm
    k1 = np.arange(SEQ_N1)
    s2i = np.arange(SEQ_N2)
    ang = 2.0 * np.pi * (k1[:, None] * s2i[None, :]) / (SEQ_N1 * SEQ_N2)
    nblk = SEQ_N2 // FN1_S2_PER_STEP

    def blocked(t):
        return t.reshape(SEQ_N1, nblk, FN1_S2_PER_STEP).transpose(1, 0, 2)

    return (jnp.asarray(chan, F32).astype(BF16), jnp.asarray(w1, F32).astype(BF16),
            jnp.asarray(w2, F32).astype(BF16),
            jnp.asarray(blocked(np.cos(ang)), F32), jnp.asarray(blocked(np.sin(ang)), F32))


def kernel(x, ffn1_w_gate, ffn1_w_up, ffn1_w_down, ln1_g, ln1_b, w_in, rpb, g_attn, g_fourier,
           w_out, ln2_g, ln2_b, ffn2_w_gate, ffn2_w_up, ffn2_w_down, ln3_g, ln3_b):
    batch, seq, d = x.shape
    assert d == D_MODEL and seq == SEQ_N1 * SEQ_N2 and seq % (ATT_PAIRS_PER_STEP * PAIR_Q) == 0
    assert rpb.shape[0] == DEPTH
    chan, w1, w2, tcos, tsin = _dft_constants()

    def row(p, l):
        return p[l].reshape(1, -1).astype(F32)

    wg1, wu1, wd1, win, wout, wg2, wu2, wd2 = _cast_weights(
        [ffn1_w_gate, ffn1_w_up, ffn1_w_down, w_in, w_out, ffn2_w_gate, ffn2_w_up, ffn2_w_down])

    h = x.reshape(batch * seq, d).astype(F32)
    for l in range(DEPTH):
        h, q, k, v, u = _ffn_proj(l, h, wg1, wu1, wd1, row(ln1_g, l), row(ln1_b, l), win)
        attn = _nattn(q, k, v, _toeplitz_tables(rpb[l]), batch, seq)
        four = _fnet(u, chan, w1, w2, tcos, tsin, batch, seq)
        h = _mix_ffn(l, attn, four, h, row(g_attn, l), row(g_fourier, l), wout,
                     row(ln2_g, l), row(ln2_b, l), wg2, wu2, wd2, row(ln3_g, l), row(ln3_b, l))
    return h.reshape(batch, seq, d).astype(x.dtype)
```

```python
import functools
import math

import jax
import jax.numpy as jnp
import numpy as np
from jax import lax
from jax.experimental import pallas as pl
from jax.experimental.pallas import tpu as pltpu

D_MODEL = 1024
DEPTH = 2
GRID_W = 64
WIN_ROWS = 8
WIN_COLS = 16
N_ATTN_HEADS = 8
HEAD_DIM = 64
D_ATTN = N_ATTN_HEADS * HEAD_DIM
N_FOURIER_GROUPS = 4
FOURIER_GROUP = 128
D_FOURIER = N_FOURIER_GROUPS * FOURIER_GROUP
D_IN = 3 * D_ATTN + D_FOURIER
D_FF = 2816
ALPHA = (2.0 * DEPTH) ** 0.25
LN_EPS = 1e-5
RMS_EPS = 1e-6
NEG_INF = -1e30

V7X_LANES = 128
V7X_MXU_DIM = 256
V7X_VMEM_BYTES = 64 * 1024 * 1024

F32 = jnp.float32
BF16 = jnp.bfloat16


def _cparams(vmem_bytes, n_axes):
    return pltpu.CompilerParams(
        dimension_semantics=("arbitrary",) * n_axes,
        vmem_limit_bytes=min(int(vmem_bytes), V7X_VMEM_BYTES - (2 << 20)),
    )


def _const_spec(shape):
    nd = len(shape)
    return pl.BlockSpec(shape, lambda *_: (0,) * nd, pipeline_mode=pl.Buffered(1))


def _layer_norm(z, g, b):
    mu = jnp.mean(z, axis=-1, keepdims=True)
    zc = z - mu
    var = jnp.mean(zc * zc, axis=-1, keepdims=True)
    return zc * lax.rsqrt(var + LN_EPS) * g + b


def _rms_norm(z, g):
    return z * lax.rsqrt(jnp.mean(z * z, axis=-1, keepdims=True) + RMS_EPS) * g


def _dot(a, b):
    return jnp.dot(a, b, preferred_element_type=F32)


TOK_TM = 1024
MIX_TM = 1024
FFN_FC = V7X_MXU_DIM
Q_SCALE = HEAD_DIM ** -0.5 * math.log2(math.e)


TOK_SUB = 256
N_SUB = TOK_TM // TOK_SUB


def _sub_rows(s):
    return slice(s * TOK_SUB, (s + 1) * TOK_SUB)


def _swiglu_ln(xs, wg_ref, wu_ref, wd_ref, g_ref, b_ref, h_ref):
    xbs = [x.astype(BF16) for x in xs]
    for c in range(D_FF // FFN_FC):
        cols = slice(c * FFN_FC, (c + 1) * FFN_FC)
        for s, xb in enumerate(xbs):
            gate = _dot(xb, wg_ref[:, cols])
            up = _dot(xb, wu_ref[:, cols])
            h_ref[s, :, cols] = (gate * jax.nn.sigmoid(gate) * up).astype(BF16)
    outs = []
    for s, x in enumerate(xs):
        y = _dot(h_ref[s], wd_ref[...])
        outs.append(_layer_norm(ALPHA * x + 0.5 * y, g_ref[...], b_ref[...]))
    return outs


def _ffn_proj_kernel(x_ref, wg_ref, wu_ref, wd_ref, g_ref, b_ref, win_ref,
                     x1_ref, q_ref, k_ref, v_ref, u_ref, h_ref):
    x1s = _swiglu_ln([x_ref[_sub_rows(s), :] for s in range(N_SUB)],
                     wg_ref, wu_ref, wd_ref, g_ref, b_ref, h_ref)
    for s, x1 in enumerate(x1s):
        rows = _sub_rows(s)
        x1_ref[rows, :] = x1
        xb = x1.astype(BF16)
        q_ref[rows, :] = (_dot(xb, win_ref[:, 0:D_ATTN]) * Q_SCALE).astype(BF16)
        k_ref[rows, :] = _dot(xb, win_ref[:, D_ATTN:2 * D_ATTN]).astype(BF16)
        v_ref[rows, :] = _dot(xb, win_ref[:, 2 * D_ATTN:3 * D_ATTN]).astype(BF16)
        u = _dot(xb, win_ref[:, 3 * D_ATTN:])
        for grp in range(N_FOURIER_GROUPS):
            u_ref[grp, rows, :] = u[:, grp * FOURIER_GROUP:(grp + 1) * FOURIER_GROUP]


def _mix_ffn_kernel(attn_ref, four_ref, x_ref, ga_ref, gf_ref, wo_ref, g2_ref, b2_ref,
                    wg_ref, wu_ref, wd_ref, g3_ref, b3_ref, o_ref, h_ref):
    x2s = []
    for s in range(MIX_TM // TOK_SUB):
        rows = _sub_rows(s)
        ma = _rms_norm(attn_ref[rows, :], ga_ref[...]).astype(BF16)
        mf = _rms_norm(four_ref[rows, :], gf_ref[...]).astype(BF16)
        z = _dot(ma, wo_ref[0:D_ATTN, :]) + _dot(mf, wo_ref[D_ATTN:, :])
        x2s.append(_layer_norm(ALPHA * x_ref[rows, :] + z, g2_ref[...], b2_ref[...]))
    outs = _swiglu_ln(x2s, wg_ref, wu_ref, wd_ref, g3_ref, b3_ref, h_ref)
    for s, out in enumerate(outs):
        o_ref[_sub_rows(s), :] = out


def _tok_spec(width, tm=TOK_TM):
    return pl.BlockSpec((tm, width), lambda i: (i, 0))


def _layer_spec(layer, rows, cols):
    return pl.BlockSpec((None, rows, cols), lambda i: (layer, 0, 0), pipeline_mode=pl.Buffered(1))


def _ffn_weight_specs(layer):
    return [_layer_spec(layer, D_MODEL, D_FF), _layer_spec(layer, D_MODEL, D_FF),
            _layer_spec(layer, D_FF, D_MODEL), _const_spec((1, D_MODEL)), _const_spec((1, D_MODEL))]


CAST_STEPS = 16


def _cast_kernel(*refs):
    n = len(refs) // 2
    for src, dst in zip(refs[:n], refs[n:]):
        dst[...] = src[...].astype(BF16)


def _cast_weights(weights):
    flat = [w.reshape(-1, w.shape[-1]) for w in weights]
    specs = [pl.BlockSpec((f.shape[0] // CAST_STEPS, f.shape[1]), lambda i: (i, 0)) for f in flat]
    vmem = 2 * sum(f.size // CAST_STEPS * 6 for f in flat) + (4 << 20)
    outs = pl.pallas_call(
        _cast_kernel,
        grid=(CAST_STEPS,),
        in_specs=specs,
        out_specs=specs,
        out_shape=[jax.ShapeDtypeStruct(f.shape, BF16) for f in flat],
        compiler_params=_cparams(vmem, 1),
        name="cast_weights",
    )(*flat)
    return [o.reshape(w.shape) for o, w in zip(outs, weights)]


def _ffn_proj(layer, x, wg, wu, wd, g, b, w_in):
    n = x.shape[0]
    half = jax.ShapeDtypeStruct((n, D_ATTN), BF16)
    four = jax.ShapeDtypeStruct((N_FOURIER_GROUPS, n, FOURIER_GROUP), F32)
    four_spec = pl.BlockSpec((N_FOURIER_GROUPS, TOK_TM, FOURIER_GROUP), lambda i: (0, i, 0))
    vmem = (3 * D_MODEL * D_FF * 2 + D_MODEL * D_IN * 2 + TOK_TM * D_FF * 2
            + 2 * TOK_TM * (2 * D_MODEL * 4 + 3 * D_ATTN * 2 + D_FOURIER * 4)
            + 8 * TOK_SUB * D_MODEL * 4)
    return pl.pallas_call(
        _ffn_proj_kernel,
        grid=(n // TOK_TM,),
        in_specs=([_tok_spec(D_MODEL)] + _ffn_weight_specs(layer)
                  + [_layer_spec(layer, D_MODEL, D_IN)]),
        out_specs=[_tok_spec(D_MODEL)] + [_tok_spec(D_ATTN)] * 3 + [four_spec],
        out_shape=[jax.ShapeDtypeStruct((n, D_MODEL), F32)] + [half] * 3 + [four],
        scratch_shapes=[pltpu.VMEM((N_SUB, TOK_SUB, D_FF), BF16)],
        compiler_params=_cparams(vmem, 1),
        name="ffn_proj",
    )(x, wg, wu, wd, g, b, w_in)


def _mix_ffn(layer, attn, four, x, ga, gf, wo, g2, b2, wg, wu, wd, g3, b3):
    n = x.shape[0]
    tm = MIX_TM
    vmem = (3 * D_MODEL * D_FF * 2 + D_MODEL * D_MODEL * 2 + tm * D_FF * 2
            + 2 * tm * (2 * D_MODEL * 4 + D_ATTN * 4 + D_FOURIER * 4)
            + 16 * TOK_SUB * D_MODEL * 4)
    return pl.pallas_call(
        _mix_ffn_kernel,
        grid=(n // tm,),
        in_specs=([_tok_spec(D_ATTN, tm), _tok_spec(D_FOURIER, tm), _tok_spec(D_MODEL, tm),
                   _const_spec((1, D_ATTN)), _const_spec((1, D_FOURIER)),
                   _layer_spec(layer, D_MODEL, D_MODEL), _const_spec((1, D_MODEL)),
                   _const_spec((1, D_MODEL))]
                  + _ffn_weight_specs(layer)),
        out_specs=_tok_spec(D_MODEL, tm),
        out_shape=jax.ShapeDtypeStruct((n, D_MODEL), F32),
        scratch_shapes=[pltpu.VMEM((tm // TOK_SUB, TOK_SUB, D_FF), BF16)],
        compiler_params=_cparams(vmem, 1),
        name="mix_ffn",
    )(attn, four, x, ga, gf, wo, g2, b2, wg, wu, wd, g3, b3)


PAIR_Q = 2 * GRID_W
PAIR_KROWS = WIN_ROWS + 2
PAIR_K = PAIR_KROWS * GRID_W
ATT_PAIRS_PER_STEP = 16
ATT_PAIRS_PER_ITER = 4
N_TZ = 2 * WIN_ROWS
LOG2E = math.log2(math.e)
ATT_HALO = (WIN_ROWS // 2) * GRID_W
ATT_KV_WINDOW = ATT_PAIRS_PER_STEP * PAIR_Q + PAIR_K - PAIR_Q


def _kv_window_start(step, seq):
    tiles = jnp.clip(step * (ATT_PAIRS_PER_STEP * PAIR_Q // V7X_LANES) - ATT_HALO // V7X_LANES,
                     0, (seq - ATT_KV_WINDOW) // V7X_LANES)
    return tiles * V7X_LANES


def _toeplitz_tables(rpb_l):
    cols = np.arange(GRID_W)
    c0 = np.clip(cols - WIN_COLS // 2, 0, GRID_W - WIN_COLS)
    in_win = (cols[None, :] >= c0[:, None]) & (cols[None, :] < c0[:, None] + WIN_COLS)
    dc_idx = np.clip(cols[None, :] - cols[:, None], -(WIN_COLS - 1), WIN_COLS - 1) + (WIN_COLS - 1)
    onehot = (dc_idx[None] == np.arange(2 * WIN_COLS - 1)[:, None, None]).astype(np.float32)
    tz = jnp.einsum("hdj,jqk->hdqk", rpb_l.astype(F32) * LOG2E, jnp.asarray(onehot),
                    precision=lax.Precision.HIGHEST)
    tz = jnp.where(in_win, tz, NEG_INF)
    tzp = jnp.pad(tz, ((0, 0), (1, 1), (0, 0), (0, 0)), constant_values=NEG_INF)
    left, right = tzp[:, :-1], tzp[:, 1:]
    masked = jnp.full_like(left, NEG_INF)
    return jnp.concatenate([jnp.concatenate([left, right], axis=-1),
                            jnp.concatenate([left, masked], axis=-1),
                            jnp.concatenate([masked, right], axis=-1)], axis=1)


def _nattn_kernel(q_ref, k_ref, v_ref, tz_ref, o_ref, s_ref, *, n_rows):
    step = pl.program_id(1)
    kv_start = _kv_window_start(step, n_rows * GRID_W)
    lane = lax.broadcasted_iota(jnp.int32, (PAIR_Q, V7X_LANES), 1)
    first_head = lane < HEAD_DIM
    ones_block = jnp.ones((PAIR_K, V7X_LANES), BF16)

    def pair_geometry(pl_idx):
        p = step * ATT_PAIRS_PER_STEP + pl_idx
        w = jnp.clip(2 * p - WIN_ROWS // 2, 0, n_rows - PAIR_KROWS)

        def tz_index(a, t):
            r = 2 * p + a
            r0 = jnp.clip(r - WIN_ROWS // 2, 0, n_rows - WIN_ROWS)
            kr = w + 2 * t
            ok0 = (kr >= r0) & (kr < r0 + WIN_ROWS)
            ok1 = (kr + 1 >= r0) & (kr + 1 < r0 + WIN_ROWS)
            i = jnp.clip(kr - r + WIN_ROWS, 0, N_TZ - 1)
            return jnp.where(ok0 & ok1, i,
                             jnp.where(ok0, N_TZ + i, jnp.where(ok1, 2 * N_TZ + i, N_TZ)))

        return dict(
            qrows=pl.ds(pl.multiple_of(pl_idx * PAIR_Q, PAIR_Q), PAIR_Q),
            krows=pl.ds(pl.multiple_of(w * GRID_W - kv_start, V7X_LANES), PAIR_K),
            tz_idx=[[tz_index(a, t) for t in range(PAIR_KROWS // 2)] for a in (0, 1)])

    def scores(geo, hp):
        lanes = slice(hp * V7X_LANES, (hp + 1) * V7X_LANES)
        qp = q_ref[geo["qrows"], lanes]
        kp = k_ref[geo["krows"], lanes]
        zero = jnp.zeros_like(qp)
        qq = jnp.concatenate([jnp.where(first_head, qp, zero),
                              jnp.where(first_head, zero, qp)], axis=0)
        return lax.dot_general(qq, kp, (((1,), (1,)), ((), ())),
                               preferred_element_type=F32)

    def finish(geo, hp, s):
        lanes = slice(hp * V7X_LANES, (hp + 1) * V7X_LANES)
        probs = []
        for e in (0, 1):
            h = 2 * hp + e
            bias = jnp.concatenate(
                [jnp.concatenate([tz_ref[h, geo["tz_idx"][a][t]] for t in range(PAIR_KROWS // 2)],
                                 axis=1) for a in (0, 1)], axis=0)
            se = s[e * PAIR_Q:(e + 1) * PAIR_Q] + bias
            m = jnp.max(se, axis=-1, keepdims=True)
            probs.append(jnp.exp2(se - m).astype(BF16))
        v_ext = jnp.concatenate([v_ref[geo["krows"], lanes], ones_block], axis=1)
        o = _dot(jnp.concatenate(probs, axis=0), v_ext)
        o = o[:, :V7X_LANES] / o[:, V7X_LANES:]
        o_ref[geo["qrows"], lanes] = jnp.where(first_head, o[:PAIR_Q], o[PAIR_Q:])

    def body(it, carry):
        geos = [pair_geometry(it * ATT_PAIRS_PER_ITER + j) for j in range(ATT_PAIRS_PER_ITER)]
        units = [(geos[j], hp) for j in range(ATT_PAIRS_PER_ITER)
                 for hp in range(N_ATTN_HEADS // 2)]
        s_cur = s_ref[...]
        for u in range(1, len(units)):
            s_next = scores(*units[u])
            finish(*units[u - 1], s_cur)
            s_cur = s_next
        nxt = jnp.minimum((it + 1) * ATT_PAIRS_PER_ITER, ATT_PAIRS_PER_STEP - 1)
        s_ref[...] = scores(pair_geometry(nxt), 0)
        finish(*units[-1], s_cur)
        return carry

    s_ref[...] = scores(pair_geometry(0), 0)
    lax.fori_loop(0, ATT_PAIRS_PER_STEP // ATT_PAIRS_PER_ITER, body, 0)


def _nattn(q, k, v, tz2, batch, seq):
    n_rows = seq // GRID_W
    tq = ATT_PAIRS_PER_STEP * PAIR_Q
    steps = seq // tq
    k3 = k.reshape(batch, seq, D_ATTN)
    v3 = v.reshape(batch, seq, D_ATTN)
    vmem = (2 * 2 * ATT_KV_WINDOW * D_ATTN * 2
            + tz2.size * 4
            + 2 * tq * D_ATTN * (2 + 4)
            + (8 << 20))
    kv_spec = pl.BlockSpec(
        (None, pl.Element(ATT_KV_WINDOW), pl.Element(D_ATTN)),
        lambda b, i: (b, _kv_window_start(i, seq), 0))
    return pl.pallas_call(
        functools.partial(_nattn_kernel, n_rows=n_rows),
        grid=(batch, steps),
        in_specs=[
            pl.BlockSpec((tq, D_ATTN), lambda b, i: (b * steps + i, 0)),
            kv_spec,
            kv_spec,
            _const_spec(tz2.shape),
        ],
        out_specs=pl.BlockSpec((tq, D_ATTN), lambda b, i: (b * steps + i, 0)),
        out_shape=jax.ShapeDtypeStruct((batch * seq, D_ATTN), F32),
        scratch_shapes=[pltpu.VMEM((2 * PAIR_Q, PAIR_K), F32)],
        compiler_params=_cparams(vmem, 2),
        name="nattn",
    )(q, k3, v3, tz2)


SEQ_N1 = 128
SEQ_N2 = 64
FN1_S2_PER_STEP = 8
FN2_K1_PER_STEP = 16
FN1_ROWS = SEQ_N1 * FN1_S2_PER_STEP


FN1_STEPS = SEQ_N2 // FN1_S2_PER_STEP
P_SLAB_PITCH = SEQ_N1 + 8
FN2_STEPS = SEQ_N1 // FN2_K1_PER_STEP


def _fnet_kernel(u_ref, cs_ref, w1_ref, tc_ref, ts_ref, w2_ref, y_ref, pr_ref, pi_ref):
    j = pl.program_id(1)

    @pl.when(j < FN1_STEPS)
    def _stage1():
        u2 = u_ref.reshape(N_FOURIER_GROUPS * FN1_ROWS, FOURIER_GROUP)
        tcos = tc_ref[...]
        tsin = ts_ref[...]

        def rows(grp, jj):
            return pl.ds(grp * FN1_ROWS + jj, SEQ_N1, stride=FN1_S2_PER_STEP)

        for jj in range(FN1_S2_PER_STEP):
            ug = jnp.concatenate([u2[rows(grp, jj), :] for grp in range(N_FOURIER_GROUPS)],
                                 axis=0).astype(BF16)
            ab = _dot(ug, cs_ref[...])
            ab = [ab[grp * SEQ_N1:(grp + 1) * SEQ_N1] for grp in range(N_FOURIER_GROUPS)]
            x = jnp.concatenate(
                [jnp.concatenate([t[:, :FOURIER_GROUP] for t in ab], axis=1),
                 jnp.concatenate([t[:, FOURIER_GROUP:] for t in ab], axis=1)],
                axis=0).astype(BF16)
            t = _dot(w1_ref[...], x)
            tr = t[:SEQ_N1]
            ti = t[SEQ_N1:]
            c = tcos[:, jj:jj + 1]
            s = tsin[:, jj:jj + 1]
            pr = tr * c + ti * s
            pi = ti * c - tr * s
            for grp in range(N_FOURIER_GROUPS):
                cols = slice(grp * FOURIER_GROUP, (grp + 1) * FOURIER_GROUP)
                slab0 = (grp * FN1_S2_PER_STEP + jj) * P_SLAB_PITCH
                pr_ref[j, pl.ds(slab0, SEQ_N1), :] = pr[:, cols]
                pi_ref[j, pl.ds(slab0, SEQ_N1), :] = pi[:, cols]

    @pl.when(j >= FN1_STEPS)
    def _stage2():
        kb = j - FN1_STEPS

        def slab(ref, kk):
            def tile(grp, blk):
                row0 = grp * FN1_S2_PER_STEP * P_SLAB_PITCH + kb * FN2_K1_PER_STEP + kk
                return ref[blk, pl.ds(row0, FN1_S2_PER_STEP, stride=P_SLAB_PITCH), :]
            return jnp.concatenate(
                [jnp.concatenate([tile(grp, blk) for blk in range(FN1_STEPS)], axis=0)
                 for grp in range(N_FOURIER_GROUPS)], axis=1).astype(BF16)

        for kk in range(FN2_K1_PER_STEP):
            x = jnp.concatenate([slab(pr_ref, kk), slab(pi_ref, kk)], axis=0)
            y_ref[:, kk, :] = _dot(w2_ref[...], x)


def _fnet(u, chan, w1, w2, tcos, tsin, batch, seq):
    shape5 = (N_FOURIER_GROUPS, batch, SEQ_N1, SEQ_N2, FOURIER_GROUP)
    u5 = u.reshape(shape5)
    last1 = FN1_STEPS - 1
    in_spec = pl.BlockSpec((N_FOURIER_GROUPS, None, SEQ_N1, FN1_S2_PER_STEP, FOURIER_GROUP),
                           lambda bb, j: (0, bb, 0, jnp.minimum(j, last1), 0))
    tw_spec = pl.BlockSpec((None, SEQ_N1, FN1_S2_PER_STEP),
                           lambda bb, j: (jnp.minimum(j, last1), 0, 0))
    out_spec = pl.BlockSpec((None, SEQ_N2, FN2_K1_PER_STEP, D_FOURIER),
                            lambda bb, j: (bb, 0, jnp.maximum(j - FN1_STEPS, 0), 0))
    scratch_rows = N_FOURIER_GROUPS * FN1_S2_PER_STEP * P_SLAB_PITCH
    scratch = pltpu.VMEM((FN1_STEPS, scratch_rows, FOURIER_GROUP), F32)
    blk_bytes = N_FOURIER_GROUPS * FN1_ROWS * FOURIER_GROUP * 4
    vmem = (2 * FN1_STEPS * scratch_rows * FOURIER_GROUP * 4
            + 2 * blk_bytes
            + 2 * SEQ_N2 * FN2_K1_PER_STEP * D_FOURIER * 4
            + (8 << 20))
    y = pl.pallas_call(
        _fnet_kernel,
        grid=(batch, FN1_STEPS + FN2_STEPS),
        in_specs=[in_spec, _const_spec((FOURIER_GROUP, 2 * FOURIER_GROUP)),
                  _const_spec((2 * SEQ_N1, 2 * SEQ_N1)), tw_spec, tw_spec,
                  _const_spec((SEQ_N2, 2 * SEQ_N2))],
        out_specs=out_spec,
        out_shape=jax.ShapeDtypeStruct((batch, SEQ_N2, SEQ_N1, D_FOURIER), F32),
        scratch_shapes=[scratch, scratch],
        compiler_params=_cparams(vmem, 2),
        name="fnet",
    )(u5, chan, w1, tcos, tsin, w2)
    return y.reshape(batch * seq, D_FOURIER)


def _dft_constants():
    def cs(n):
        idx = np.arange(n)
        ang = 2.0 * np.pi * ((idx[:, None] * idx[None, :]) % n) / n
        return np.cos(ang), np.sin(ang)

    cc, sc = cs(FOURIER_GROUP)
    chan = np.concatenate([cc, sc], axis=1)
    c1, s1 = cs(SEQ_N1)
    w1 = np.block([[c1, -s1], [-s1, -c1]])
    c2, s2 = cs(SEQ_N2)
    norm = 1.0 / math.sqrt(SEQ_N1 * SEQ_N2 * FOURIER_GROUP)
    w2 = np.concatenate([c2, s2], axis=1) * norm
    k1 = np.arange(SEQ_N1)
    s2i = np.arange(SEQ_N2)
    ang = 2.0 * np.pi * (k1[:, None] * s2i[None, :]) / (SEQ_N1 * SEQ_N2)
    nblk = SEQ_N2 // FN1_S2_PER_STEP

    def blocked(t):
        return t.reshape(SEQ_N1, nblk, FN1_S2_PER_STEP).transpose(1, 0, 2)

    return (jnp.asarray(chan, F32).astype(BF16), jnp.asarray(w1, F32).astype(BF16),
            jnp.asarray(w2, F32).astype(BF16),
            jnp.asarray(blocked(np.cos(ang)), F32), jnp.asarray(blocked(np.sin(ang)), F32))


def kernel(x, ffn1_w_gate, ffn1_w_up, ffn1_w_down, ln1_g, ln1_b, w_in, rpb, g_attn, g_fourier,
           w_out, ln2_g, ln2_b, ffn2_w_gate, ffn2_w_up, ffn2_w_down, ln3_g, ln3_b):
    batch, seq, d = x.shape
    assert d == D_MODEL and seq == SEQ_N1 * SEQ_N2 and seq % (ATT_PAIRS_PER_STEP * PAIR_Q) == 0
    assert rpb.shape[0] == DEPTH
    chan, w1, w2, tcos, tsin = _dft_constants()

    def row(p, l):
        return p[l].reshape(1, -1).astype(F32)

    wg1, wu1, wd1, win, wout, wg2, wu2, wd2 = _cast_weights(
        [ffn1_w_gate, ffn1_w_up, ffn1_w_down, w_in, w_out, ffn2_w_gate, ffn2_w_up, ffn2_w_down])

    h = x.reshape(batch * seq, d).astype(F32)
    for l in range(DEPTH):
        h, q, k, v, u = _ffn_proj(l, h, wg1, wu1, wd1, row(ln1_g, l), row(ln1_b, l), win)
        attn = _nattn(q, k, v, _toeplitz_tables(rpb[l]), batch, seq)
        four = _fnet(u, chan, w1, w2, tcos, tsin, batch, seq)
        h = _mix_ffn(l, attn, four, h, row(g_attn, l), row(g_fourier, l), wout,
                     row(ln2_g, l), row(ln2_b, l), wg2, wu2, wd2, row(ln3_g, l), row(ln3_b, l))
    return h.reshape(batch, seq, d).astype(x.dtype)
```

```python
import functools
import math

import jax
import jax.numpy as jnp
import numpy as np
from jax import lax
from jax.experimental import pallas as pl
from jax.experimental.pallas import tpu as pltpu

D_MODEL = 1024
DEPTH = 2
GRID_W = 64
WIN_ROWS = 8
WIN_COLS = 16
N_ATTN_HEADS = 8
HEAD_DIM = 64
D_ATTN = N_ATTN_HEADS * HEAD_DIM
N_FOURIER_GROUPS = 4
FOURIER_GROUP = 128
D_FOURIER = N_FOURIER_GROUPS * FOURIER_GROUP
D_IN = 3 * D_ATTN + D_FOURIER
D_FF = 2816
ALPHA = (2.0 * DEPTH) ** 0.25
LN_EPS = 1e-5
RMS_EPS = 1e-6
NEG_INF = -1e30

V7X_LANES = 128
V7X_MXU_DIM = 256
V7X_VMEM_BYTES = 64 * 1024 * 1024

F32 = jnp.float32
BF16 = jnp.bfloat16


def _cparams(vmem_bytes, n_axes):
    return pltpu.CompilerParams(
        dimension_semantics=("arbitrary",) * n_axes,
        vmem_limit_bytes=min(int(vmem_bytes), V7X_VMEM_BYTES - (2 << 20)),
    )


def _const_spec(shape):
    nd = len(shape)
    return pl.BlockSpec(shape, lambda *_: (0,) * nd, pipeline_mode=pl.Buffered(1))


def _layer_norm(z, g, b):
    mu = jnp.mean(z, axis=-1, keepdims=True)
    zc = z - mu
    var = jnp.mean(zc * zc, axis=-1, keepdims=True)
    return zc * lax.rsqrt(var + LN_EPS) * g + b


def _rms_norm(z, g):
    return z * lax.rsqrt(jnp.mean(z * z, axis=-1, keepdims=True) + RMS_EPS) * g


def _dot(a, b):
    return jnp.dot(a, b, preferred_element_type=F32)


TOK_TM = 1024
MIX_TM = 1024
FFN_FC = V7X_MXU_DIM
Q_SCALE = HEAD_DIM ** -0.5 * math.log2(math.e)


TOK_SUB = 256
N_SUB = TOK_TM // TOK_SUB


def _sub_rows(s):
    return slice(s * TOK_SUB, (s + 1) * TOK_SUB)


def _swiglu_ln(xs, wg_ref, wu_ref, wd_ref, g_ref, b_ref, h_ref):
    xbs = [x.astype(BF16) for x in xs]
    for c in range(D_FF // FFN_FC):
        cols = slice(c * FFN_FC, (c + 1) * FFN_FC)
        for s, xb in enumerate(xbs):
            gate = _dot(xb, wg_ref[:, cols])
            up = _dot(xb, wu_ref[:, cols])
            h_ref[s, :, cols] = (gate * jax.nn.sigmoid(gate) * up).astype(BF16)
    outs = []
    for s, x in enumerate(xs):
        y = _dot(h_ref[s], wd_ref[...])
        outs.append(_layer_norm(ALPHA * x + 0.5 * y, g_ref[...], b_ref[...]))
    return outs


def _ffn_proj_kernel(x_ref, wg_ref, wu_ref, wd_ref, g_ref, b_ref, win_ref,
                     x1_ref, q_ref, k_ref, v_ref, u_ref, h_ref):
    x1s = _swiglu_ln([x_ref[_sub_rows(s), :] for s in range(N_SUB)],
                     wg_ref, wu_ref, wd_ref, g_ref, b_ref, h_ref)
    for s, x1 in enumerate(x1s):
        rows = _sub_rows(s)
        x1_ref[rows, :] = x1
        xb = x1.astype(BF16)
        q_ref[rows, :] = (_dot(xb, win_ref[:, 0:D_ATTN]) * Q_SCALE).astype(BF16)
        k_ref[rows, :] = _dot(xb, win_ref[:, D_ATTN:2 * D_ATTN]).astype(BF16)
        v_ref[rows, :] = _dot(xb, win_ref[:, 2 * D_ATTN:3 * D_ATTN]).astype(BF16)
        u = _dot(xb, win_ref[:, 3 * D_ATTN:])
        for grp in range(N_FOURIER_GROUPS):
            u_ref[grp, rows, :] = u[:, grp * FOURIER_GROUP:(grp + 1) * FOURIER_GROUP]


def _mix_ffn_kernel(attn_ref, four_ref, x_ref, ga_ref, gf_ref, wo_ref, g2_ref, b2_ref,
                    wg_ref, wu_ref, wd_ref, g3_ref, b3_ref, o_ref, h_ref):
    x2s = []
    for s in range(MIX_TM // TOK_SUB):
        rows = _sub_rows(s)
        ma = _rms_norm(attn_ref[rows, :], ga_ref[...]).astype(BF16)
        mf = _rms_norm(four_ref[rows, :], gf_ref[...]).astype(BF16)
        z = _dot(ma, wo_ref[0:D_ATTN, :]) + _dot(mf, wo_ref[D_ATTN:, :])
        x2s.append(_layer_norm(ALPHA * x_ref[rows, :] + z, g2_ref[...], b2_ref[...]))
    outs = _swiglu_ln(x2s, wg_ref, wu_ref, wd_ref, g3_ref, b3_ref, h_ref)
    for s, out in enumerate(outs):
        o_ref[_sub_rows(s), :] = out


def _tok_spec(width, tm=TOK_TM):
    return pl.BlockSpec((tm, width), lambda i: (i, 0))


def _layer_spec(layer, rows, cols):
    return pl.BlockSpec((None, rows, cols), lambda i: (layer, 0, 0), pipeline_mode=pl.Buffered(1))


def _ffn_weight_specs(layer):
    return [_layer_spec(layer, D_MODEL, D_FF), _layer_spec(layer, D_MODEL, D_FF),
            _layer_spec(layer, D_FF, D_MODEL), _const_spec((1, D_MODEL)), _const_spec((1, D_MODEL))]


CAST_STEPS = 16


def _cast_kernel(*refs):
    n = len(refs) // 2
    for src, dst in zip(refs[:n], refs[n:]):
        dst[...] = src[...].astype(BF16)


def _cast_weights(weights):
    flat = [w.reshape(-1, w.shape[-1]) for w in weights]
    specs = [pl.BlockSpec((f.shape[0] // CAST_STEPS, f.shape[1]), lambda i: (i, 0)) for f in flat]
    vmem = 2 * sum(f.size // CAST_STEPS * 6 for f in flat) + (4 << 20)
    outs = pl.pallas_call(
        _cast_kernel,
        grid=(CAST_STEPS,),
        in_specs=specs,
        out_specs=specs,
        out_shape=[jax.ShapeDtypeStruct(f.shape, BF16) for f in flat],
        compiler_params=_cparams(vmem, 1),
        name="cast_weights",
    )(*flat)
    return [o.reshape(w.shape) for o, w in zip(outs, weights)]


def _ffn_proj(layer, x, wg, wu, wd, g, b, w_in):
    n = x.shape[0]
    half = jax.ShapeDtypeStruct((n, D_ATTN), BF16)
    four = jax.ShapeDtypeStruct((N_FOURIER_GROUPS, n, FOURIER_GROUP), F32)
    four_spec = pl.BlockSpec((N_FOURIER_GROUPS, TOK_TM, FOURIER_GROUP), lambda i: (0, i, 0))
    vmem = (3 * D_MODEL * D_FF * 2 + D_MODEL * D_IN * 2 + TOK_TM * D_FF * 2
            + 2 * TOK_TM * (2 * D_MODEL * 4 + 3 * D_ATTN * 2 + D_FOURIER * 4)
            + 8 * TOK_SUB * D_MODEL * 4)
    return pl.pallas_call(
        _ffn_proj_kernel,
        grid=(n // TOK_TM,),
        in_specs=([_tok_spec(D_MODEL)] + _ffn_weight_specs(layer)
                  + [_layer_spec(layer, D_MODEL, D_IN)]),
        out_specs=[_tok_spec(D_MODEL)] + [_tok_spec(D_ATTN)] * 3 + [four_spec],
        out_shape=[jax.ShapeDtypeStruct((n, D_MODEL), F32)] + [half] * 3 + [four],
        scratch_shapes=[pltpu.VMEM((N_SUB, TOK_SUB, D_FF), BF16)],
        compiler_params=_cparams(vmem, 1),
        name="ffn_proj",
    )(x, wg, wu, wd, g, b, w_in)


def _mix_ffn(layer, attn, four, x, ga, gf, wo, g2, b2, wg, wu, wd, g3, b3):
    n = x.shape[0]
    tm = MIX_TM
    vmem = (3 * D_MODEL * D_FF * 2 + D_MODEL * D_MODEL * 2 + tm * D_FF * 2
            + 2 * tm * (2 * D_MODEL * 4 + D_ATTN * 4 + D_FOURIER * 4)
            + 16 * TOK_SUB * D_MODEL * 4)
    return pl.pallas_call(
        _mix_ffn_kernel,
        grid=(n // tm,),
        in_specs=([_tok_spec(D_ATTN, tm), _tok_spec(D_FOURIER, tm), _tok_spec(D_MODEL, tm),
                   _const_spec((1, D_ATTN)), _const_spec((1, D_FOURIER)),
                   _layer_spec(layer, D_MODEL, D_MODEL), _const_spec((1, D_MODEL)),
                   _const_spec((1, D_MODEL))]
                  + _ffn_weight_specs(layer)),
        out_specs=_tok_spec(D_MODEL, tm),
        out_shape=jax.ShapeDtypeStruct((n, D_MODEL), F32),
        scratch_shapes=[pltpu.VMEM((tm // TOK_SUB, TOK_SUB, D_FF), BF16)],
        compiler_params=_cparams(vmem, 1),
        name="mix_ffn",
    )(attn, four, x, ga, gf, wo, g2, b2, wg, wu, wd, g3, b3)


PAIR_Q = 2 * GRID_W
PAIR_KROWS = WIN_ROWS + 2
PAIR_K = PAIR_KROWS * GRID_W
ATT_PAIRS_PER_STEP = 16
ATT_PAIRS_PER_ITER = 4
N_TZ = 2 * WIN_ROWS
LOG2E = math.log2(math.e)
ATT_HALO = (WIN_ROWS // 2) * GRID_W
ATT_KV_WINDOW = ATT_PAIRS_PER_STEP * PAIR_Q + PAIR_K - PAIR_Q


def _kv_window_start(step, seq):
    tiles = jnp.clip(step * (ATT_PAIRS_PER_STEP * PAIR_Q // V7X_LANES) - ATT_HALO // V7X_LANES,
                     0, (seq - ATT_KV_WINDOW) // V7X_LANES)
    return tiles * V7X_LANES


def _toeplitz_tables(rpb_l):
    cols = np.arange(GRID_W)
    c0 = np.clip(cols - WIN_COLS // 2, 0, GRID_W - WIN_COLS)
    in_win = (cols[None, :] >= c0[:, None]) & (cols[None, :] < c0[:, None] + WIN_COLS)
    dc_idx = np.clip(cols[None, :] - cols[:, None], -(WIN_COLS - 1), WIN_COLS - 1) + (WIN_COLS - 1)
    onehot = (dc_idx[None] == np.arange(2 * WIN_COLS - 1)[:, None, None]).astype(np.float32)
    tz = jnp.einsum("hdj,jqk->hdqk", rpb_l.astype(F32) * LOG2E, jnp.asarray(onehot),
                    precision=lax.Precision.HIGHEST)
    tz = jnp.where(in_win, tz, NEG_INF)
    tzp = jnp.pad(tz, ((0, 0), (1, 1), (0, 0), (0, 0)), constant_values=NEG_INF)
    left, right = tzp[:, :-1], tzp[:, 1:]
    masked = jnp.full_like(left, NEG_INF)
    return jnp.concatenate([jnp.concatenate([left, right], axis=-1),
                            jnp.concatenate([left, masked], axis=-1),
                            jnp.concatenate([masked, right], axis=-1)], axis=1)


def _nattn_kernel(q_ref, k_ref, v_ref, tz_ref, o_ref, s_ref, *, n_rows):
    step = pl.program_id(1)
    kv_start = _kv_window_start(step, n_rows * GRID_W)
    lane = lax.broadcasted_iota(jnp.int32, (PAIR_Q, V7X_LANES), 1)
    first_head = lane < HEAD_DIM
    ones_block = jnp.ones((PAIR_K, V7X_LANES), BF16)

    def pair_geometry(pl_idx):
        p = step * ATT_PAIRS_PER_STEP + pl_idx
        w = jnp.clip(2 * p - WIN_ROWS // 2, 0, n_rows - PAIR_KROWS)

        def tz_index(a, t):
            r = 2 * p + a
            r0 = jnp.clip(r - WIN_ROWS // 2, 0, n_rows - WIN_ROWS)
            kr = w + 2 * t
            ok0 = (kr >= r0) & (kr < r0 + WIN_ROWS)
            ok1 = (kr + 1 >= r0) & (kr + 1 < r0 + WIN_ROWS)
            i = jnp.clip(kr - r + WIN_ROWS, 0, N_TZ - 1)
            return jnp.where(ok0 & ok1, i,
                             jnp.where(ok0, N_TZ + i, jnp.where(ok1, 2 * N_TZ + i, N_TZ)))

        return dict(
            qrows=pl.ds(pl.multiple_of(pl_idx * PAIR_Q, PAIR_Q), PAIR_Q),
            krows=pl.ds(pl.multiple_of(w * GRID_W - kv_start, V7X_LANES), PAIR_K),
            tz_idx=[[tz_index(a, t) for t in range(PAIR_KROWS // 2)] for a in (0, 1)])

    def scores(geo, hp):
        lanes = slice(hp * V7X_LANES, (hp + 1) * V7X_LANES)
        qp = q_ref[geo["qrows"], lanes]
        kp = k_ref[geo["krows"], lanes]
        zero = jnp.zeros_like(qp)
        qq = jnp.concatenate([jnp.where(first_head, qp, zero),
                              jnp.where(first_head, zero, qp)], axis=0)
        return lax.dot_general(qq, kp, (((1,), (1,)), ((), ())),
                               preferred_element_type=F32)

    def finish(geo, hp, s):
        lanes = slice(hp * V7X_LANES, (hp + 1) * V7X_LANES)
        probs = []
        for e in (0, 1):
            h = 2 * hp + e
            bias = jnp.concatenate(
                [jnp.concatenate([tz_ref[h, geo["tz_idx"][a][t]] for t in range(PAIR_KROWS // 2)],
                                 axis=1) for a in (0, 1)], axis=0)
            se = s[e * PAIR_Q:(e + 1) * PAIR_Q] + bias
            m = jnp.max(se, axis=-1, keepdims=True)
            probs.append(jnp.exp2(se - m).astype(BF16))
        v_ext = jnp.concatenate([v_ref[geo["krows"], lanes], ones_block], axis=1)
        o = _dot(jnp.concatenate(probs, axis=0), v_ext)
        o = o[:, :V7X_LANES] / o[:, V7X_LANES:]
        o_ref[geo["qrows"], lanes] = jnp.where(first_head, o[:PAIR_Q], o[PAIR_Q:])

    def body(it, carry):
        geos = [pair_geometry(it * ATT_PAIRS_PER_ITER + j) for j in range(ATT_PAIRS_PER_ITER)]
        units = [(geos[j], hp) for j in range(ATT_PAIRS_PER_ITER)
                 for hp in range(N_ATTN_HEADS // 2)]
        s_cur = s_ref[...]
        for u in range(1, len(units)):
            s_next = scores(*units[u])
            finish(*units[u - 1], s_cur)
            s_cur = s_next
        nxt = jnp.minimum((it + 1) * ATT_PAIRS_PER_ITER, ATT_PAIRS_PER_STEP - 1)
        s_ref[...] = scores(pair_geometry(nxt), 0)
        finish(*units[-1], s_cur)
        return carry

    s_ref[...] = scores(pair_geometry(0), 0)
    lax.fori_loop(0, ATT_PAIRS_PER_STEP // ATT_PAIRS_PER_ITER, body, 0)


def _nattn(q, k, v, tz2, batch, seq):
    n_rows = seq // GRID_W
    tq = ATT_PAIRS_PER_STEP * PAIR_Q
    steps = seq // tq
    k3 = k.reshape(batch, seq, D_ATTN)
    v3 = v.reshape(batch, seq, D_ATTN)
    vmem = (2 * 2 * ATT_KV_WINDOW * D_ATTN * 2
            + tz2.size * 4
            + 2 * tq * D_ATTN * (2 + 4)
            + (8 << 20))
    kv_spec = pl.BlockSpec(
        (None, pl.Element(ATT_KV_WINDOW), pl.Element(D_ATTN)),
        lambda b, i: (b, _kv_window_start(i, seq), 0))
    return pl.pallas_call(
        functools.partial(_nattn_kernel, n_rows=n_rows),
        grid=(batch, steps),
        in_specs=[
            pl.BlockSpec((tq, D_ATTN), lambda b, i: (b * steps + i, 0)),
            kv_spec,
            kv_spec,
            _const_spec(tz2.shape),
        ],
        out_specs=pl.BlockSpec((tq, D_ATTN), lambda b, i: (b * steps + i, 0)),
        out_shape=jax.ShapeDtypeStruct((batch * seq, D_ATTN), F32),
        scratch_shapes=[pltpu.VMEM((2 * PAIR_Q, PAIR_K), F32)],
        compiler_params=_cparams(vmem, 2),
        name="nattn",
    )(q, k3, v3, tz2)


SEQ_N1 = 128
SEQ_N2 = 64
FN1_S2_PER_STEP = 8
FN2_K1_PER_STEP = 16
FN1_ROWS = SEQ_N1 * FN1_S2_PER_STEP


FN1_STEPS = SEQ_N2 // FN1_S2_PER_STEP
P_SLAB_PITCH = SEQ_N1 + 8
FN2_STEPS = SEQ_N1 // FN2_K1_PER_STEP


def _fnet_kernel(u_ref, cs_ref, w1_ref, tc_ref, ts_ref, w2_ref, y_ref, pr_ref, pi_ref):
    j = pl.program_id(1)

    @pl.when(j < FN1_STEPS)
    def _stage1():
        u2 = u_ref.reshape(N_FOURIER_GROUPS * FN1_ROWS, FOURIER_GROUP)
        tcos = tc_ref[...]
        tsin = ts_ref[...]

        def rows(grp, jj):
            return pl.ds(grp * FN1_ROWS + jj, SEQ_N1, stride=FN1_S2_PER_STEP)

        for jj in range(FN1_S2_PER_STEP):
            ug = jnp.concatenate([u2[rows(grp, jj), :] for grp in range(N_FOURIER_GROUPS)],
                                 axis=0).astype(BF16)
            ab = _dot(ug, cs_ref[...])
            ab = [ab[grp * SEQ_N1:(grp + 1) * SEQ_N1] for grp in range(N_FOURIER_GROUPS)]
            x = jnp.concatenate(
                [jnp.concatenate([t[:, :FOURIER_GROUP] for t in ab], axis=1),
                 jnp.concatenate([t[:, FOURIER_GROUP:] for t in ab], axis=1)],
                axis=0).astype(BF16)
            t = _dot(w1_ref[...], x)
            tr = t[:SEQ_N1]
            ti = t[SEQ_N1:]
            c = tcos[:, jj:jj + 1]
            s = tsin[:, jj:jj + 1]
            pr = tr * c + ti * s
            pi = ti * c - tr * s
            for grp in range(N_FOURIER_GROUPS):
                cols = slice(grp * FOURIER_GROUP, (grp + 1) * FOURIER_GROUP)
                slab0 = (grp * FN1_S2_PER_STEP + jj) * P_SLAB_PITCH
                pr_ref[j, pl.ds(slab0, SEQ_N1), :] = pr[:, cols]
                pi_ref[j, pl.ds(slab0, SEQ_N1), :] = pi[:, cols]

    @pl.when(j >= FN1_STEPS)
    def _stage2():
        kb = j - FN1_STEPS

        def slab(ref, kk):
            def tile(grp, blk):
                row0 = grp * FN1_S2_PER_STEP * P_SLAB_PITCH + kb * FN2_K1_PER_STEP + kk
                return ref[blk, pl.ds(row0, FN1_S2_PER_STEP, stride=P_SLAB_PITCH), :]
            return jnp.concatenate(
                [jnp.concatenate([tile(grp, blk) for blk in range(FN1_STEPS)], axis=0)
                 for grp in range(N_FOURIER_GROUPS)], axis=1).astype(BF16)

        x = jnp.concatenate(
            [jnp.concatenate([slab(pr_ref, kk), slab(pi_ref, kk)], axis=0)
             for kk in range(FN2_K1_PER_STEP)], axis=1)
        res = _dot(w2_ref[...], x)
        for kk in range(FN2_K1_PER_STEP):
            y_ref[:, kk, :] = res[:, kk * D_FOURIER:(kk + 1) * D_FOURIER]


def _fnet(u, chan, w1, w2, tcos, tsin, batch, seq):
    shape5 = (N_FOURIER_GROUPS, batch, SEQ_N1, SEQ_N2, FOURIER_GROUP)
    u5 = u.reshape(shape5)
    last1 = FN1_STEPS - 1
    in_spec = pl.BlockSpec((N_FOURIER_GROUPS, None, SEQ_N1, FN1_S2_PER_STEP, FOURIER_GROUP),
                           lambda bb, j: (0, bb, 0, jnp.minimum(j, last1), 0))
    tw_spec = pl.BlockSpec((None, SEQ_N1, FN1_S2_PER_STEP),
                           lambda bb, j: (jnp.minimum(j, last1), 0, 0))
    out_spec = pl.BlockSpec((None, SEQ_N2, FN2_K1_PER_STEP, D_FOURIER),
                            lambda bb, j: (bb, 0, jnp.maximum(j - FN1_STEPS, 0), 0))
    scratch_rows = N_FOURIER_GROUPS * FN1_S2_PER_STEP * P_SLAB_PITCH
    scratch = pltpu.VMEM((FN1_STEPS, scratch_rows, FOURIER_GROUP), F32)
    blk_bytes = N_FOURIER_GROUPS * FN1_ROWS * FOURIER_GROUP * 4
    vmem = (2 * FN1_STEPS * scratch_rows * FOURIER_GROUP * 4
            + 2 * blk_bytes
            + 2 * SEQ_N2 * FN2_K1_PER_STEP * D_FOURIER * 4
            + (8 << 20))
    y = pl.pallas_call(
        _fnet_kernel,
        grid=(batch, FN1_STEPS + FN2_STEPS),
        in_specs=[in_spec, _const_spec((FOURIER_GROUP, 2 * FOURIER_GROUP)),
                  _const_spec((2 * SEQ_N1, 2 * SEQ_N1)), tw_spec, tw_spec,
                  _const_spec((SEQ_N2, 2 * SEQ_N2))],
        out_specs=out_spec,
        out_shape=jax.ShapeDtypeStruct((batch, SEQ_N2, SEQ_N1, D_FOURIER), F32),
        scratch_shapes=[scratch, scratch],
        compiler_params=_cparams(vmem, 2),
        name="fnet",
    )(u5, chan, w1, tcos, tsin, w2)
    return y.reshape(batch * seq, D_FOURIER)


def _dft_constants():
    def cs(n):
        idx = np.arange(n)
        ang = 2.0 * np.pi * ((idx[:, None] * idx[None, :]) % n) / n
        return np.cos(ang), np.sin(ang)

    cc, sc = cs(FOURIER_GROUP)
    chan = np.concatenate([cc, sc], axis=1)
    c1, s1 = cs(SEQ_N1)
    w1 = np.block([[c1, -s1], [-s1, -c1]])
    c2, s2 = cs(SEQ_N2)
    norm = 1.0 / math.sqrt(SEQ_N1 * SEQ_N2 * FOURIER_GROUP)
    w2 = np.concatenate([c2, s2], axis=1) * norm
    k1 = np.arange(SEQ_N1)
    s2i = np.arange(SEQ_N2)
    ang = 2.0 * np.pi * (k1[:, None] * s2i[None, :]) / (SEQ_N1 * SEQ_N2)
    nblk = SEQ_N2 // FN1_S2_PER_STEP

    def blocked(t):
        return t.reshape(SEQ_N1, nblk, FN1_S2_PER_STEP).transpose(1, 0, 2)

    return (jnp.asarray(chan, F32).astype(BF16), jnp.asarray(w1, F32).astype(BF16),
            jnp.asarray(w2, F32).astype(BF16),
            jnp.asarray(blocked(np.cos(ang)), F32), jnp.asarray(blocked(np.sin(ang)), F32))


def kernel(x, ffn1_w_gate, ffn1_w_up, ffn1_w_down, ln1_g, ln1_b, w_in, rpb, g_attn, g_fourier,
           w_out, ln2_g, ln2_b, ffn2_w_gate, ffn2_w_up, ffn2_w_down, ln3_g, ln3_b):
    batch, seq, d = x.shape
    assert d == D_MODEL and seq == SEQ_N1 * SEQ_N2 and seq % (ATT_PAIRS_PER_STEP * PAIR_Q) == 0
    assert rpb.shape[0] == DEPTH
    chan, w1, w2, tcos, tsin = _dft_constants()

    def row(p, l):
        return p[l].reshape(1, -1).astype(F32)

    wg1, wu1, wd1, win, wout, wg2, wu2, wd2 = _cast_weights(
        [ffn1_w_gate, ffn1_w_up, ffn1_w_down, w_in, w_out, ffn2_w_gate, ffn2_w_up, ffn2_w_down])

    h = x.reshape(batch * seq, d).astype(F32)
    for l in range(DEPTH):
        h, q, k, v, u = _ffn_proj(l, h, wg1, wu1, wd1, row(ln1_g, l), row(ln1_b, l), win)
        attn = _nattn(q, k, v, _toeplitz_tables(rpb[l]), batch, seq)
        four = _fnet(u, chan, w1, w2, tcos, tsin, batch, seq)
        h = _mix_ffn(l, attn, four, h, row(g_attn, l), row(g_fourier, l), wout,
                     row(ln2_g, l), row(ln2_b, l), wg2, wu2, wd2, row(ln3_g, l), row(ln3_b, l))
    return h.reshape(batch, seq, d).astype(x.dtype)
```
